```python
import math
import jax, jax.numpy as jnp
from jax import lax
import numpy as np

D_MODEL = 1024
BATCH = 2
SEQ = 8192
DEPTH = 2

N_EVEN = (DEPTH + 1) // 2
N_ODD = DEPTH // 2

S5_WIDTH = D_MODEL // 2
S5_GROUP = 16
S5_GROUPS = S5_WIDTH // S5_GROUP
S5_STATE = 64
S5_MIN_STEP = 0.001
S5_MAX_STEP = 0.1
SB_HEAD_DIM = 64
SB_HEADS = (D_MODEL // 2) // SB_HEAD_DIM
SB_WIDTH = SB_HEADS * SB_HEAD_DIM
EVEN_IN = S5_WIDTH + 3 * SB_WIDTH
EVEN_MIX = S5_WIDTH + SB_WIDTH
FOX_HEAD_DIM = 64
FOX_HEADS = D_MODEL // FOX_HEAD_DIM
FOX_WIDTH = FOX_HEADS * FOX_HEAD_DIM
ODD_IN = 3 * FOX_WIDTH + FOX_HEADS
D_FF = 2816
N_EXPERTS = 8
TOP_K = 2
D_FF_EXPERT = 3584
Q_BLOCK = 128
RMS_EPS = 1e-6

kernel_name = "hybrid_s5_stickbreak_fox_moe"

F32 = jnp.float32


def rmsnorm(x, g):
    xf = x.astype(F32)
    y = xf * lax.rsqrt(jnp.mean(xf * xf, axis=-1, keepdims=True) + RMS_EPS)
    return (y * g.astype(F32)).astype(x.dtype)


def to_heads(t, n_heads, head_dim):
    b, l, _ = t.shape
    return t.reshape(b, l, n_heads, head_dim).transpose(0, 2, 1, 3)


def from_heads(t):
    b, h, l, d = t.shape
    return t.transpose(0, 2, 1, 3).reshape(b, l, h * d)


def s5_mixer(u, a_re, a_im, b_re, b_im, c_re, c_im, d_skip, log_step, w_glu, b_glu):
    bsz, seqlen, _ = u.shape
    uf = u.astype(F32).reshape(bsz, seqlen, S5_GROUPS, S5_GROUP)
    lam = lax.complex(a_re.astype(F32), a_im.astype(F32))
    step = jnp.exp(log_step.astype(F32))[:, None]
    lam_bar = jnp.exp(lam * step)
    b_cplx = lax.complex(b_re.astype(F32), b_im.astype(F32))
    b_bar = ((lam_bar - 1.0) / lam)[..., None] * b_cplx
    bu = jnp.einsum('gph,blgh->blgp', b_bar, uf.astype(jnp.complex64))
    a_seq = jnp.broadcast_to(lam_bar, bu.shape)

    def combine(left, right):
        a_l, s_l = left
        a_r, s_r = right
        return a_r * a_l, a_r * s_l + s_r

    _, states = lax.associative_scan(combine, (a_seq, bu), axis=1)
    c_cplx = lax.complex(c_re.astype(F32), c_im.astype(F32))
    y = jnp.einsum('ghp,blgp->blgh', c_cplx, states).real + d_skip.astype(F32) * uf
    y = jax.nn.gelu(y.reshape(bsz, seqlen, S5_WIDTH))
    out = y * jax.nn.sigmoid(y @ w_glu.astype(F32) + b_glu.astype(F32))
    return out.astype(u.dtype)


def stick_breaking_attention(q, k, v):
    bsz, nh, seqlen, dh = q.shape
    nblk = seqlen // Q_BLOCK
    scale = 1.0 / math.sqrt(dh)
    qb = q.reshape(bsz, nh, nblk, Q_BLOCK, dh).transpose(2, 0, 1, 3, 4)
    kpos = jnp.arange(seqlen)

    def block(args):
        qi, blk = args
        qpos = blk * Q_BLOCK + jnp.arange(Q_BLOCK)
        z = jnp.einsum('bhqd,bhkd->bhqk', qi, k).astype(F32) * scale
        strict = kpos[None, :] < qpos[:, None]
        log_keep = jnp.where(strict, jax.nn.log_sigmoid(-z), 0.0)
        after = lax.cumsum(log_keep, axis=3, reverse=True) - log_keep
        w = jnp.where(strict, jnp.exp(jax.nn.log_sigmoid(z) + after), 0.0)
        return jnp.einsum('bhqk,bhkd->bhqd', w.astype(v.dtype), v)

    out = lax.map(block, (qb, jnp.arange(nblk)))
    return out.transpose(1, 2, 0, 3, 4).reshape(bsz, nh, seqlen, dh)


def forgetting_attention(q, k, v, log_f):
    bsz, nh, seqlen, dh = q.shape
    nblk = seqlen // Q_BLOCK
    scale = 1.0 / math.sqrt(dh)
    cum = lax.cumsum(log_f, axis=2)
    qb = q.reshape(bsz, nh, nblk, Q_BLOCK, dh).transpose(2, 0, 1, 3, 4)
    cb = cum.reshape(bsz, nh, nblk, Q_BLOCK).transpose(2, 0, 1, 3)
    kpos = jnp.arange(seqlen)

    def block(args):
        qi, ci, blk = args
        qpos = blk * Q_BLOCK + jnp.arange(Q_BLOCK)
        s = jnp.einsum('bhqd,bhkd->bhqk', qi, k).astype(F32) * scale
        s = s + ci[..., None] - cum[:, :, None, :]
        causal = kpos[None, :] <= qpos[:, None]
        p = jax.nn.softmax(jnp.where(causal, s, -jnp.inf), axis=-1)
        return jnp.einsum('bhqk,bhkd->bhqd', p.astype(v.dtype), v)

    out = lax.map(block, (qb, cb, jnp.arange(nblk)))
    return out.transpose(1, 2, 0, 3, 4).reshape(bsz, nh, seqlen, dh)


def swiglu(h, w_gate, w_up, w_down):
    return (jax.nn.silu(h @ w_gate) * (h @ w_up)) @ w_down


def moe_swiglu(h, w_router, w_gate, w_up, w_down):
    logits = (h @ w_router).astype(F32)
    top_val, top_idx = lax.top_k(logits, TOP_K)
    gates = jax.nn.softmax(top_val, axis=-1)
    combine = jnp.sum(jax.nn.one_hot(top_idx, N_EXPERTS, dtype=F32) * gates[..., None], axis=-2)
    out = jnp.zeros_like(h)
    for e in range(N_EXPERTS):
        y_e = swiglu(h, w_gate[e], w_up[e], w_down[e])
        out = out + combine[..., e:e + 1].astype(h.dtype) * y_e
    return out


def setup_inputs(seed: int = 0) -> dict:
    key = jax.random.key(seed)
    ks = iter(jax.random.split(key, 40))

    def nrm(shape, scale):
        return jax.random.normal(next(ks), shape, F32) * scale

    def gain(shape):
        return 1.0 + 0.02 * jax.random.normal(next(ks), shape, F32)

    E, O = N_EVEN, N_ODD
    G, H, P = S5_GROUPS, S5_GROUP, S5_STATE
    n_idx = jnp.arange(P, dtype=F32)
    inp = {}
    inp["x"] = nrm((BATCH, SEQ, D_MODEL), 1.0)
    inp["ev_norm_mix"] = gain((E, D_MODEL))
    inp["ev_w_in"] = nrm((E, D_MODEL, EVEN_IN), D_MODEL ** -0.5)
    inp["s5_a_re"] = -0.5 + 0.01 * jax.random.normal(next(ks), (E, G, P), F32)
    inp["s5_a_im"] = math.pi * n_idx + 0.01 * jax.random.normal(next(ks), (E, G, P), F32)
    inp["s5_b_re"] = nrm((E, G, P, H), (2.0 * H) ** -0.5)
    inp["s5_b_im"] = nrm((E, G, P, H), (2.0 * H) ** -0.5)
    inp["s5_c_re"] = nrm((E, G, H, P), P ** -0.5)
    inp["s5_c_im"] = nrm((E, G, H, P), P ** -0.5)
    inp["s5_d"] = nrm((E, G, H), 1.0)
    inp["s5_log_step"] = jax.random.uniform(next(ks), (E, G), F32, math.log(S5_MIN_STEP), math.log(S5_MAX_STEP))
    inp["s5_w_glu"] = nrm((E, S5_WIDTH, S5_WIDTH), S5_WIDTH ** -0.5)
    inp["s5_b_glu"] = nrm((E, S5_WIDTH), 0.01)
    inp["ev_w_out"] = nrm((E, EVEN_MIX, D_MODEL), EVEN_MIX ** -0.5)
    inp["ev_norm_ffn"] = gain((E, D_MODEL))
    inp["ffn_w_gate"] = nrm((E, D_MODEL, D_FF), D_MODEL ** -0.5)
    inp["ffn_w_up"] = nrm((E, D_MODEL, D_FF), D_MODEL ** -0.5)
    inp["ffn_w_down"] = nrm((E, D_FF, D_MODEL), D_FF ** -0.5)
    inp["od_norm_mix"] = gain((O, D_MODEL))
    inp["od_w_in"] = nrm((O, D_MODEL, ODD_IN), D_MODEL ** -0.5)
    inp["fox_b_f"] = jax.random.uniform(next(ks), (O, FOX_HEADS), F32, 1.0, 6.0)
    inp["od_w_out"] = nrm((O, FOX_WIDTH, D_MODEL), FOX_WIDTH ** -0.5)
    inp["od_norm_ffn"] = gain((O, D_MODEL))
    inp["moe_w_router"] = nrm((O, D_MODEL, N_EXPERTS), D_MODEL ** -0.5)
    inp["moe_w_gate"] = nrm((O, N_EXPERTS, D_MODEL, D_FF_EXPERT), D_MODEL ** -0.5)
    inp["moe_w_up"] = nrm((O, N_EXPERTS, D_MODEL, D_FF_EXPERT), D_MODEL ** -0.5)
    inp["moe_w_down"] = nrm((O, N_EXPERTS, D_FF_EXPERT, D_MODEL), D_FF_EXPERT ** -0.5)
    inp["final_norm"] = gain((D_MODEL,))
    return inp


def reference(x, ev_norm_mix, ev_w_in, s5_a_re, s5_a_im, s5_b_re, s5_b_im, s5_c_re, s5_c_im,
              s5_d, s5_log_step, s5_w_glu, s5_b_glu, ev_w_out, ev_norm_ffn, ffn_w_gate, ffn_w_up,
              ffn_w_down, od_norm_mix, od_w_in, fox_b_f, od_w_out, od_norm_ffn, moe_w_router,
              moe_w_gate, moe_w_up, moe_w_down, final_norm):
    for layer in range(DEPTH):
        i = layer // 2
        if layer % 2 == 0:
            h = rmsnorm(x, ev_norm_mix[i])
            proj = h @ ev_w_in[i]
            u, q, k, v = jnp.split(proj, [S5_WIDTH, S5_WIDTH + SB_WIDTH, S5_WIDTH + 2 * SB_WIDTH], axis=-1)
            a_out = s5_mixer(u, s5_a_re[i], s5_a_im[i], s5_b_re[i], s5_b_im[i], s5_c_re[i],
                             s5_c_im[i], s5_d[i], s5_log_step[i], s5_w_glu[i], s5_b_glu[i])
            b_out = from_heads(stick_breaking_attention(to_heads(q, SB_HEADS, SB_HEAD_DIM),
                                                        to_heads(k, SB_HEADS, SB_HEAD_DIM),
                                                        to_heads(v, SB_HEADS, SB_HEAD_DIM)))
            x = x + jnp.concatenate([a_out, b_out], axis=-1) @ ev_w_out[i]
            h = rmsnorm(x, ev_norm_ffn[i])
            x = x + swiglu(h, ffn_w_gate[i], ffn_w_up[i], ffn_w_down[i])
        else:
            h = rmsnorm(x, od_norm_mix[i])
            proj = h @ od_w_in[i]
            q, k, v, f_logit = jnp.split(proj, [FOX_WIDTH, 2 * FOX_WIDTH, 3 * FOX_WIDTH], axis=-1)
            log_f = jax.nn.log_sigmoid(f_logit.astype(F32) + fox_b_f[i].astype(F32))
            c_out = from_heads(forgetting_attention(to_heads(q, FOX_HEADS, FOX_HEAD_DIM),
                                                    to_heads(k, FOX_HEADS, FOX_HEAD_DIM),
                                                    to_heads(v, FOX_HEADS, FOX_HEAD_DIM),
                                                    log_f.transpose(0, 2, 1)))
            x = x + c_out @ od_w_out[i]
            h = rmsnorm(x, od_norm_ffn[i])
            x = x + moe_swiglu(h, moe_w_router[i], moe_w_gate[i], moe_w_up[i], moe_w_down[i])
    return rmsnorm(x, final_norm)
```

```python
import functools
import math

import jax
import jax.numpy as jnp
from jax import lax
from jax.experimental import pallas as pl
from jax.experimental.pallas import tpu as pltpu

F32 = jnp.float32
BF16 = jnp.bfloat16

RMS_EPS = 1e-6
LANES = 128
VMEM_LIMIT_BYTES = 56 * 1024 * 1024

S5_GROUP = 16
S5_STATE = 64
S5_CHUNK = 16
SB_HEAD_DIM = 64
FOX_HEAD_DIM = 64
N_EXPERTS = 8
ATT_BLOCK = 256


def _params(*sem):
    return pltpu.CompilerParams(dimension_semantics=sem, vmem_limit_bytes=VMEM_LIMIT_BYTES)


def _rms(x, g):
    ms = jnp.mean(x * x, axis=-1, keepdims=True)
    return x * lax.rsqrt(ms + RMS_EPS) * g


def _dot(a, b):
    return jnp.dot(a, b, preferred_element_type=F32)


def _norm_proj_kernel(n_out, x_ref, g_ref, *refs):
    w_refs, o_refs = refs[:n_out], refs[n_out:]
    h = _rms(x_ref[...], g_ref[...]).astype(BF16)
    for w_ref, o_ref in zip(w_refs, o_refs):
        o_ref[...] = _dot(h, w_ref[...]).astype(o_ref.dtype)


def _norm_proj(x, gain, weights, out_dtypes, tm=512):
    t, d = x.shape
    n_out = len(weights)
    in_specs = [pl.BlockSpec((tm, d), lambda i: (i, 0)), pl.BlockSpec((1, d), lambda i: (0, 0))]
    in_specs += [pl.BlockSpec(w.shape, lambda i: (0, 0)) for w in weights]
    out_specs = [pl.BlockSpec((tm, w.shape[1]), lambda i: (i, 0)) for w in weights]
    out_shape = [jax.ShapeDtypeStruct((t, w.shape[1]), dt) for w, dt in zip(weights, out_dtypes)]
    return pl.pallas_call(
        functools.partial(_norm_proj_kernel, n_out),
        grid=(t // tm,), in_specs=in_specs, out_specs=out_specs, out_shape=out_shape,
        compiler_params=_params("parallel"), name="norm_proj",
    )(x, gain.reshape(1, d), *weights)


def _proj_residual_kernel(n_in, res_ref, *refs):
    a_refs, w_refs, o_ref = refs[:n_in], refs[n_in:2 * n_in], refs[2 * n_in]
    acc = res_ref[...]
    for a_ref, w_ref in zip(a_refs, w_refs):
        acc = acc + _dot(a_ref[...], w_ref[...])
    o_ref[...] = acc


def _proj_residual(res, acts, weights, tm=512):
    t, n = res.shape
    n_in = len(acts)
    in_specs = [pl.BlockSpec((tm, n), lambda i: (i, 0))]
    in_specs += [pl.BlockSpec((tm, a.shape[1]), lambda i: (i, 0)) for a in acts]
    in_specs += [pl.BlockSpec(w.shape, lambda i: (0, 0)) for w in weights]
    return pl.pallas_call(
        functools.partial(_proj_residual_kernel, n_in),
        grid=(t // tm,), in_specs=in_specs, out_specs=pl.BlockSpec((tm, n), lambda i: (i, 0)),
        out_shape=jax.ShapeDtypeStruct((t, n), F32),
        compiler_params=_params("parallel"), name="proj_residual",
    )(res, *acts, *weights)


def _s5_tables(a_re, a_im, b_re, b_im, c_re, c_im, log_step, n_scan):
    g, p = a_re.shape
    h = b_re.shape[-1]
    ch = S5_CHUNK
    lam = lax.complex(a_re, a_im)
    dl = lam * jnp.exp(log_step)[:, None]
    lam_bar = jnp.exp(dl)
    b_bar = ((lam_bar - 1.0) / lam)[..., None] * lax.complex(b_re, b_im)
    c = lax.complex(c_re, c_im)
    steps = jnp.arange(ch + 1, dtype=a_re.dtype)
    pw = jnp.exp(dl[:, None, :] * steps[None, :, None])
    kd = jnp.einsum('gop,gdp,gpi->gdoi', c, pw[:, :ch], b_bar).real
    lag = jnp.arange(ch)[None, :] - jnp.arange(ch)[:, None]
    kfull = kd[:, jnp.clip(lag, 0, ch - 1)]
    kfull = jnp.where((lag >= 0)[None, :, :, None, None], kfull, 0.0)
    k_tab = kfull.transpose(0, 1, 4, 2, 3).reshape(g, ch * h, ch * h)
    bc = pw[:, ch - 1 - jnp.arange(ch)][:, :, :, None] * b_bar[:, None]
    bc = bc.transpose(0, 1, 3, 2).reshape(g, ch * h, p)
    b_tab = jnp.concatenate([bc.real, bc.imag], axis=-1)
    m = c[:, None] * pw[:, 1:ch + 1][:, :, None, :]
    m = m.transpose(0, 3, 1, 2).reshape(g, p, ch * h)
    c_tab = jnp.concatenate([m.real, -m.imag], axis=1)
    pows = (ch * (2 ** jnp.arange(n_scan))).astype(a_re.dtype)
    ak = jnp.exp(dl[:, None, :] * pows[None, :, None])
    p1 = jnp.concatenate([ak.real, ak.real], axis=-1)
    p2 = jnp.concatenate([-ak.imag, ak.imag], axis=-1)
    return k_tab, b_tab, c_tab, p1, p2


def _s5_kernel(n_scan, rows_per_seq, u_ref, k_ref, b_ref, c_ref, p1_ref, p2_ref, y_ref):
    u = u_ref[0]
    state = _dot(u, b_ref[0])
    row = lax.broadcasted_iota(jnp.int32, state.shape, 0) & (rows_per_seq - 1)
    half = state.shape[1] // 2
    for k in range(n_scan):
        shift = 1 << k
        prev = jnp.where(row >= shift, pltpu.roll(state, shift, axis=0), 0.0)
        state = state + prev * p1_ref[0, k:k + 1, :] + pltpu.roll(prev, half, axis=1) * p2_ref[0, k:k + 1, :]
    carried = jnp.where(row >= 1, pltpu.roll(state, 1, axis=0), 0.0)
    y_ref[0] = _dot(u, k_ref[0]) + _dot(carried.astype(BF16), c_ref[0])


def _s5_conv(u, tables, batch, seqlen):
    k_tab, b_tab, c_tab, p1, p2 = tables
    t, width = u.shape
    g = width // S5_GROUP
    ch = S5_CHUNK
    rows = t // ch
    rows_per_seq = seqlen // ch
    n_scan = p1.shape[1]
    ug = u.reshape(rows, ch, g, S5_GROUP).transpose(2, 0, 1, 3).reshape(g, rows, ch * S5_GROUP).astype(BF16)
    kw = ch * S5_GROUP
    sw = b_tab.shape[-1]
    y = pl.pallas_call(
        functools.partial(_s5_kernel, n_scan, rows_per_seq),
        grid=(g,),
        in_specs=[pl.BlockSpec((1, rows, kw), lambda i: (i, 0, 0)),
                  pl.BlockSpec((1, kw, kw), lambda i: (i, 0, 0)),
                  pl.BlockSpec((1, kw, sw), lambda i: (i, 0, 0)),
                  pl.BlockSpec((1, sw, kw), lambda i: (i, 0, 0)),
                  pl.BlockSpec((1, n_scan, sw), lambda i: (i, 0, 0)),
                  pl.BlockSpec((1, n_scan, sw), lambda i: (i, 0, 0))],
        out_specs=pl.BlockSpec((1, rows, kw), lambda i: (i, 0, 0)),
        out_shape=jax.ShapeDtypeStruct((g, rows, kw), F32),
        compiler_params=_params("parallel"), name="s5_conv",
    )(ug, k_tab.astype(BF16), b_tab.astype(BF16), c_tab.astype(BF16), p1, p2)
    return y.reshape(g, rows, ch, S5_GROUP).transpose(1, 2, 0, 3).reshape(t, width)


def _gelu_tanh(x):
    return 0.5 * x * (1.0 + jnp.tanh(math.sqrt(2.0 / math.pi) * (x + 0.044715 * (x * x * x))))


def _s5_glu_kernel(y_ref, u_ref, d_ref, w_ref, b_ref, o_ref):
    y = _gelu_tanh(y_ref[...] + d_ref[...] * u_ref[...])
    gate = jax.nn.sigmoid(_dot(y.astype(BF16), w_ref[...]) + b_ref[...])
    o_ref[...] = (y * gate).astype(o_ref.dtype)


def _s5_glu(y, u, d_skip, w_glu, b_glu, tm=1024):
    t, width = y.shape
    row = pl.BlockSpec((tm, width), lambda i: (i, 0))
    vec = pl.BlockSpec((1, width), lambda i: (0, 0))
    return pl.pallas_call(
        _s5_glu_kernel, grid=(t // tm,),
        in_specs=[row, row, vec, pl.BlockSpec((width, width), lambda i: (0, 0)), vec],
        out_specs=row, out_shape=jax.ShapeDtypeStruct((t, width), BF16),
        compiler_params=_params("parallel"), name="s5_glu",
    )(y, u, d_skip.reshape(1, width), w_glu.astype(BF16), b_glu.reshape(1, width))


def _to_qt(t, batch, heads, blk):
    tt, w = t.shape
    d = w // heads
    nb = tt // batch // blk
    return t.reshape(batch, nb, blk, heads, d).transpose(0, 3, 1, 4, 2).reshape(batch * heads, nb, d, blk)


def _to_k(t, batch, heads, blk):
    tt, w = t.shape
    d = w // heads
    nb = tt // batch // blk
    return t.reshape(batch, nb, blk, heads, d).transpose(0, 3, 1, 2, 4).reshape(batch * heads, nb, blk, d)


def _from_ot(o, batch, heads):
    bh, nb, d, blk = o.shape
    return o.reshape(batch, heads, nb, d, blk).transpose(0, 2, 4, 1, 3).reshape(batch * nb * blk, heads * d)


def _block_pos(blk):
    key = lax.broadcasted_iota(jnp.int32, (blk, blk), 0)
    qry = lax.broadcasted_iota(jnp.int32, (blk, blk), 1)
    return key, qry


def _softplus(z):
    return jnp.maximum(z, 0.0) + jnp.log(1.0 + jnp.exp(-jnp.abs(z)))


def _sb_kernel(qt_ref, k_ref, vt_ref, tri_ref, ot_ref):
    nq = qt_ref.shape[1]
    blk = qt_ref.shape[3]
    dv = vt_ref.shape[2]
    key, qry = _block_pos(blk)
    strict = key < qry

    def block(qt, j, carry, acc, masked):
        z = _dot(k_ref[0, j], qt)
        sp = _softplus(z)
        if masked:
            sp = jnp.where(strict, sp, 0.0)
        hi = sp.astype(BF16)
        lo = (sp - hi.astype(F32)).astype(BF16)
        csum = _dot(tri_ref[...], hi) + _dot(tri_ref[...], lo)
        w = jnp.exp(z - csum - carry)
        if masked:
            w = jnp.where(strict, w, 0.0)
        acc = acc + _dot(vt_ref[0, j], w.astype(BF16))
        return carry + csum[0:1, :], acc

    def q_block(qi, _):
        qt = qt_ref[0, qi]
        carry = jnp.zeros((1, blk), F32)
        acc = jnp.zeros((dv, blk), F32)
        carry, acc = block(qt, qi, carry, acc, True)

        def k_block(i, c):
            return block(qt, qi - 1 - i, c[0], c[1], False)

        carry, acc = lax.fori_loop(0, qi, k_block, (carry, acc))
        ot_ref[0, qi] = acc.astype(ot_ref.dtype)
        return 0

    lax.fori_loop(0, nq, q_block, 0)


def _sb_attention(qt, k, vt):
    bh, nq, dq, blk = qt.shape
    dv = vt.shape[2]
    idx = lax.broadcasted_iota(jnp.int32, (blk, blk), 0) <= lax.broadcasted_iota(jnp.int32, (blk, blk), 1)
    tri = idx.astype(BF16)
    return pl.pallas_call(
        _sb_kernel, grid=(bh,),
        in_specs=[pl.BlockSpec((1, nq, dq, blk), lambda i: (i, 0, 0, 0)),
                  pl.BlockSpec((1, nq, blk, dq), lambda i: (i, 0, 0, 0)),
                  pl.BlockSpec((1, nq, dv, blk), lambda i: (i, 0, 0, 0)),
                  pl.BlockSpec((blk, blk), lambda i: (0, 0))],
        out_specs=pl.BlockSpec((1, nq, dv, blk), lambda i: (i, 0, 0, 0)),
        out_shape=jax.ShapeDtypeStruct((bh, nq, dv, blk), BF16),
        compiler_params=_params("parallel"), name="sb_attention",
    )(qt, k, vt, tri)


def _fox_kernel(qt_ref, k_ref, vt_ref, bias_ref, ot_ref):
    nq = qt_ref.shape[1]
    blk = qt_ref.shape[3]
    dv = vt_ref.shape[2]
    key, qry = _block_pos(blk)
    causal = key <= qry

    def block(qt, j, m, l, acc, masked):
        bias = bias_ref[0, j]
        s = _dot(k_ref[0, j], qt) + jnp.concatenate([bias] * (blk // LANES), axis=1)
        if masked:
            s = jnp.where(causal, s, -jnp.inf)
        m_new = jnp.maximum(m, jnp.max(s, axis=0, keepdims=True))
        alpha = jnp.exp(m - m_new)
        p = jnp.exp(s - m_new)
        l = alpha * l + jnp.sum(p, axis=0, keepdims=True)
        acc = alpha * acc + _dot(vt_ref[0, j], p.astype(BF16))
        return m_new, l, acc

    def q_block(qi, _):
        qt = qt_ref[0, qi]
        m = jnp.full((1, blk), -jnp.inf, F32)
        l = jnp.zeros((1, blk), F32)
        acc = jnp.zeros((dv, blk), F32)

        def k_block(j, c):
            return block(qt, j, c[0], c[1], c[2], False)

        m, l, acc = lax.fori_loop(0, qi, k_block, (m, l, acc))
        m, l, acc = block(qt, qi, m, l, acc, True)
        ot_ref[0, qi] = (acc / l).astype(ot_ref.dtype)
        return 0

    lax.fori_loop(0, nq, q_block, 0)


def _fox_attention(qt, k, vt, bias):
    bh, nq, dq, blk = qt.shape
    dv = vt.shape[2]
    return pl.pallas_call(
        _fox_kernel, grid=(bh,),
        in_specs=[pl.BlockSpec((1, nq, dq, blk), lambda i: (i, 0, 0, 0)),
                  pl.BlockSpec((1, nq, blk, dq), lambda i: (i, 0, 0, 0)),
                  pl.BlockSpec((1, nq, dv, blk), lambda i: (i, 0, 0, 0)),
                  pl.BlockSpec((1, nq, blk, LANES), lambda i: (i, 0, 0, 0))],
        out_specs=pl.BlockSpec((1, nq, dv, blk), lambda i: (i, 0, 0, 0)),
        out_shape=jax.ShapeDtypeStruct((bh, nq, dv, blk), BF16),
        compiler_params=_params("parallel"), name="fox_attention",
    )(qt, k, vt, bias)


def _split3(x):
    hi = x.astype(BF16)
    r = x - hi.astype(F32)
    mid = r.astype(BF16)
    lo = (r - mid.astype(F32)).astype(BF16)
    return hi, mid, lo


def _fox_gate_kernel(f_ref, b_ref, tri_ref, c_ref, carry_ref):
    @pl.when(pl.program_id(1) == 0)
    def _():
        carry_ref[...] = jnp.zeros_like(carry_ref)

    x = f_ref[...] + b_ref[...]
    log_f = jnp.minimum(x, 0.0) - jnp.log(1.0 + jnp.exp(-jnp.abs(x)))
    hi, mid, lo = _split3(log_f)
    tri = tri_ref[...]
    c = _dot(tri, hi) + _dot(tri, mid) + _dot(tri, lo) + carry_ref[...]
    c_ref[...] = c
    carry_ref[...] = c[c.shape[0] - 1:, :]


def _fox_gate_cumsum(f_logit, bias, batch, seqlen, tc=512):
    t, w = f_logit.shape
    nchunk = seqlen // tc
    tri = (lax.broadcasted_iota(jnp.int32, (tc, tc), 1) <= lax.broadcasted_iota(jnp.int32, (tc, tc), 0)).astype(BF16)
    return pl.pallas_call(
        _fox_gate_kernel, grid=(batch, nchunk),
        in_specs=[pl.BlockSpec((tc, w), lambda b, c: (b * nchunk + c, 0)),
                  pl.BlockSpec((1, w), lambda b, c: (0, 0)),
                  pl.BlockSpec((tc, tc), lambda b, c: (0, 0))],
        out_specs=pl.BlockSpec((tc, w), lambda b, c: (b * nchunk + c, 0)),
        out_shape=jax.ShapeDtypeStruct((t, w), F32),
        scratch_shapes=[pltpu.VMEM((1, w), F32)],
        compiler_params=_params("parallel", "arbitrary"), name="fox_gate_cumsum",
    )(f_logit, bias, tri)


def _router_kernel(x_ref, g_ref, whi_ref, wlo_ref, o_ref):
    h = _rms(x_ref[...], g_ref[...])
    h_hi = h.astype(BF16)
    h_lo = (h - h_hi.astype(F32)).astype(BF16)
    logits = _dot(h_hi, whi_ref[...]) + _dot(h_lo, whi_ref[...]) + _dot(h_hi, wlo_ref[...])
    lane = lax.broadcasted_iota(jnp.int32, logits.shape, 1)
    logits = jnp.where(lane < N_EXPERTS, logits, -jnp.inf)
    m1 = jnp.max(logits, axis=1, keepdims=True)
    i1 = jnp.min(jnp.where(logits == m1, lane, LANES), axis=1, keepdims=True)
    rest = jnp.where(lane == i1, -jnp.inf, logits)
    m2 = jnp.max(rest, axis=1, keepdims=True)
    i2 = jnp.min(jnp.where(rest == m2, lane, LANES), axis=1, keepdims=True)
    e2 = jnp.exp(m2 - m1)
    g1 = 1.0 / (1.0 + e2)
    o_ref[...] = jnp.where(lane == i1, g1, 0.0) + jnp.where(lane == i2, e2 * g1, 0.0)


def _router(x, gain, w_router, tm=1024):
    t, d = x.shape
    w = jnp.zeros((d, LANES), F32).at[:, :N_EXPERTS].set(w_router)
    w_hi = w.astype(BF16)
    w_lo = (w - w_hi.astype(F32)).astype(BF16)
    return pl.pallas_call(
        _router_kernel, grid=(t // tm,),
        in_specs=[pl.BlockSpec((tm, d), lambda i: (i, 0)), pl.BlockSpec((1, d), lambda i: (0, 0)),
                  pl.BlockSpec((d, LANES), lambda i: (0, 0)), pl.BlockSpec((d, LANES), lambda i: (0, 0))],
        out_specs=pl.BlockSpec((tm, LANES), lambda i: (i, 0)),
        out_shape=jax.ShapeDtypeStruct((t, LANES), F32),
        compiler_params=_params("parallel"), name="router",
    )(x, gain.reshape(1, d), w_hi, w_lo)


def _swiglu_kernel(has_comb, has_final, x_ref, g_ref, *refs):
    refs = list(refs)
    comb_ref = refs.pop(0) if has_comb else None
    wg_ref, wu_ref, wd_ref = refs[:3]
    refs = refs[3:]
    fg_ref = refs.pop(0) if has_final else None
    o_ref, h_ref, acc_ref = refs
    e, f = pl.program_id(1), pl.program_id(2)
    first = jnp.logical_and(e == 0, f == 0)
    last = jnp.logical_and(e == pl.num_programs(1) - 1, f == pl.num_programs(2) - 1)

    @pl.when(first)
    def _():
        h_ref[...] = _rms(x_ref[...], g_ref[...]).astype(BF16)
        acc_ref[...] = jnp.zeros_like(acc_ref)

    h = h_ref[...]
    gate = _dot(h, wg_ref[0])
    up = _dot(h, wu_ref[0])
    act = gate * jax.nn.sigmoid(gate) * up
    if has_comb:
        lane = lax.broadcasted_iota(jnp.int32, comb_ref.shape, 1)
        act = act * jnp.sum(jnp.where(lane == e, comb_ref[...], 0.0), axis=1, keepdims=True)
    acc_ref[...] += _dot(act.astype(BF16), wd_ref[0])

    @pl.when(last)
    def _():
        out = x_ref[...] + acc_ref[...]
        if has_final:
            out = _rms(out, fg_ref[...])
        o_ref[...] = out


def _swiglu(x, gain, w_gate, w_up, w_down, comb=None, final_gain=None, tm=1024, tf=256):
    t, d = x.shape
    n_e, _, ff = w_gate.shape
    row = pl.BlockSpec((tm, d), lambda i, e, f: (i, 0))
    vec = pl.BlockSpec((1, d), lambda i, e, f: (0, 0))
    in_specs, args = [row, vec], [x, gain.reshape(1, d)]
    if comb is not None:
        in_specs.append(pl.BlockSpec((tm, LANES), lambda i, e, f: (i, 0)))
        args.append(comb)
    in_specs += [pl.BlockSpec((1, d, tf), lambda i, e, f: (e, 0, f)),
                 pl.BlockSpec((1, d, tf), lambda i, e, f: (e, 0, f)),
                 pl.BlockSpec((1, tf, d), lambda i, e, f: (e, f, 0))]
    args += [w_gate, w_up, w_down]
    if final_gain is not None:
        in_specs.append(vec)
        args.append(final_gain.reshape(1, d))
    return pl.pallas_call(
        functools.partial(_swiglu_kernel, comb is not None, final_gain is not None),
        grid=(t // tm, n_e, ff // tf), in_specs=in_specs, out_specs=row,
        out_shape=jax.ShapeDtypeStruct((t, d), F32),
        scratch_shapes=[pltpu.VMEM((tm, d), BF16), pltpu.VMEM((tm, d), F32)],
        compiler_params=_params("parallel", "arbitrary", "arbitrary"), name="swiglu",
    )(*args)


def _even_layer(x, batch, seqlen, norm_mix, w_in, s5, w_glu, b_glu, d_skip, w_out, norm_ffn, w_gate, w_up, w_down):
    s5_width = d_skip.size
    sb_width = (w_in.shape[1] - s5_width) // 3
    heads = sb_width // SB_HEAD_DIM
    scale = 1.0 / math.sqrt(SB_HEAD_DIM)
    w_u = w_in[:, :s5_width].astype(BF16)
    w_q = (w_in[:, s5_width:s5_width + sb_width] * scale).astype(BF16)
    w_kv = w_in[:, s5_width + sb_width:].astype(BF16)
    u, q, kv = _norm_proj(x, norm_mix, [w_u, w_q, w_kv], [F32, BF16, BF16])

    n_scan = int(math.log2(seqlen // S5_CHUNK))
    tables = _s5_tables(*s5, n_scan)
    y = _s5_conv(u, tables, batch, seqlen)
    a_out = _s5_glu(y, u, d_skip, w_glu, b_glu)

    qt = _to_qt(q, batch, heads, ATT_BLOCK)
    k = _to_k(kv[:, :sb_width], batch, heads, ATT_BLOCK)
    vt = _to_qt(kv[:, sb_width:], batch, heads, ATT_BLOCK)
    b_out = _from_ot(_sb_attention(qt, k, vt), batch, heads)

    w_out = w_out.astype(BF16)
    x = _proj_residual(x, [a_out, b_out], [w_out[:s5_width], w_out[s5_width:]])
    return _swiglu(x, norm_ffn, w_gate.astype(BF16)[None], w_up.astype(BF16)[None], w_down.astype(BF16)[None], tf=256)


def _odd_layer(x, batch, seqlen, norm_mix, w_in, b_f, w_out, norm_ffn, w_router, w_gate, w_up, w_down, final_gain):
    heads = b_f.size
    width = heads * FOX_HEAD_DIM
    scale = 1.0 / math.sqrt(FOX_HEAD_DIM)
    d = x.shape[1]
    w_q = (w_in[:, :width] * scale).astype(BF16)
    w_kv = w_in[:, width:3 * width].astype(BF16)
    w_f = jnp.zeros((d, LANES), F32).at[:, :heads].set(w_in[:, 3 * width:]).astype(BF16)
    q, kv, f_logit = _norm_proj(x, norm_mix, [w_q, w_kv, w_f], [BF16, BF16, F32])
    bias = jnp.zeros((1, LANES), F32).at[0, :heads].set(b_f)
    cum = _fox_gate_cumsum(f_logit, bias, batch, seqlen)[:, :heads]

    nb = seqlen // ATT_BLOCK
    bias = (-cum).reshape(batch, nb, ATT_BLOCK, heads).transpose(0, 3, 1, 2).reshape(batch * heads, nb, ATT_BLOCK, 1)
    bias = jnp.broadcast_to(bias, (batch * heads, nb, ATT_BLOCK, LANES))

    qt = _to_qt(q, batch, heads, ATT_BLOCK)
    k = _to_k(kv[:, :width], batch, heads, ATT_BLOCK)
    vt = _to_qt(kv[:, width:], batch, heads, ATT_BLOCK)
    c_out = _from_ot(_fox_attention(qt, k, vt, bias), batch, heads)

    x = _proj_residual(x, [c_out], [w_out.astype(BF16)])
    comb = _router(x, norm_ffn, w_router)
    return _swiglu(x, norm_ffn, w_gate.astype(BF16), w_up.astype(BF16), w_down.astype(BF16),
                   comb=comb, final_gain=final_gain, tf=512)


def kernel(x, ev_norm_mix, ev_w_in, s5_a_re, s5_a_im, s5_b_re, s5_b_im, s5_c_re, s5_c_im, s5_d, s5_log_step,
           s5_w_glu, s5_b_glu, ev_w_out, ev_norm_ffn, ffn_w_gate, ffn_w_up, ffn_w_down, od_norm_mix, od_w_in,
           fox_b_f, od_w_out, od_norm_ffn, moe_w_router, moe_w_gate, moe_w_up, moe_w_down, final_norm):
    batch, seqlen, d = x.shape
    n_even, n_odd = ev_w_in.shape[0], od_w_in.shape[0]
    depth = n_even + n_odd
    h = x.reshape(batch * seqlen, d)
    for layer in range(depth):
        i = layer // 2
        last = layer == depth - 1
        if layer % 2 == 0:
            s5 = (s5_a_re[i], s5_a_im[i], s5_b_re[i], s5_b_im[i], s5_c_re[i], s5_c_im[i], s5_log_step[i])
            h = _even_layer(h, batch, seqlen, ev_norm_mix[i], ev_w_in[i], s5, s5_w_glu[i], s5_b_glu[i],
                            s5_d[i].reshape(-1), ev_w_out[i], ev_norm_ffn[i], ffn_w_gate[i], ffn_w_up[i],
                            ffn_w_down[i])
        else:
            h = _odd_layer(h, batch, seqlen, od_norm_mix[i], od_w_in[i], fox_b_f[i], od_w_out[i], od_norm_ffn[i],
                           moe_w_router[i], moe_w_gate[i], moe_w_up[i], moe_w_down[i],
                           final_norm if last else None)
    return h.reshape(batch, seqlen, d)
```

```python
import functools
import math

import jax
import jax.numpy as jnp
from jax import lax
from jax.experimental import pallas as pl
from jax.experimental.pallas import tpu as pltpu

F32 = jnp.float32
BF16 = jnp.bfloat16

RMS_EPS = 1e-6
LANES = 128
VMEM_LIMIT_BYTES = 56 * 1024 * 1024

S5_GROUP = 16
S5_STATE = 64
S5_CHUNK = 16
SB_HEAD_DIM = 64
FOX_HEAD_DIM = 64
N_EXPERTS = 8
MXU_TILE = 256
ATT_BQ = 1024
ATT_BK = 512


def _params(*sem):
    return pltpu.CompilerParams(dimension_semantics=sem, vmem_limit_bytes=VMEM_LIMIT_BYTES)


def _rms(x, g):
    ms = jnp.mean(x * x, axis=-1, keepdims=True)
    return x * lax.rsqrt(ms + RMS_EPS) * g


def _dot(a, b):
    return jnp.dot(a, b, preferred_element_type=F32)


def _norm_proj_kernel(n_out, x_ref, g_ref, *refs):
    w_refs, o_refs = refs[:n_out], refs[n_out:]
    h = _rms(x_ref[...], g_ref[...]).astype(BF16)
    for w_ref, o_ref in zip(w_refs, o_refs):
        o_ref[...] = _dot(h, w_ref[...]).astype(o_ref.dtype)


def _norm_proj(x, gain, weights, out_dtypes, tm=512):
    t, d = x.shape
    n_out = len(weights)
    in_specs = [pl.BlockSpec((tm, d), lambda i: (i, 0)), pl.BlockSpec((1, d), lambda i: (0, 0))]
    in_specs += [pl.BlockSpec(w.shape, lambda i: (0, 0)) for w in weights]
    out_specs = [pl.BlockSpec((tm, w.shape[1]), lambda i: (i, 0)) for w in weights]
    out_shape = [jax.ShapeDtypeStruct((t, w.shape[1]), dt) for w, dt in zip(weights, out_dtypes)]
    return pl.pallas_call(
        functools.partial(_norm_proj_kernel, n_out),
        grid=(t // tm,), in_specs=in_specs, out_specs=out_specs, out_shape=out_shape,
        compiler_params=_params("parallel"), name="norm_proj",
    )(x, gain.reshape(1, d), *weights)


def _proj_residual_kernel(n_in, res_ref, *refs):
    a_refs, w_refs, o_ref = refs[:n_in], refs[n_in:2 * n_in], refs[2 * n_in]
    acc = res_ref[...]
    for a_ref, w_ref in zip(a_refs, w_refs):
        acc = acc + _dot(a_ref[...], w_ref[...])
    o_ref[...] = acc


def _proj_residual(res, acts, weights, tm=512):
    t, n = res.shape
    n_in = len(acts)
    in_specs = [pl.BlockSpec((tm, n), lambda i: (i, 0))]
    in_specs += [pl.BlockSpec((tm, a.shape[1]), lambda i: (i, 0)) for a in acts]
    in_specs += [pl.BlockSpec(w.shape, lambda i: (0, 0)) for w in weights]
    return pl.pallas_call(
        functools.partial(_proj_residual_kernel, n_in),
        grid=(t // tm,), in_specs=in_specs, out_specs=pl.BlockSpec((tm, n), lambda i: (i, 0)),
        out_shape=jax.ShapeDtypeStruct((t, n), F32),
        compiler_params=_params("parallel"), name="proj_residual",
    )(res, *acts, *weights)


def _s5_tables(a_re, a_im, b_re, b_im, c_re, c_im, log_step, n_scan):
    g, p = a_re.shape
    h = b_re.shape[-1]
    ch = S5_CHUNK
    lam = lax.complex(a_re, a_im)
    dl = lam * jnp.exp(log_step)[:, None]
    lam_bar = jnp.exp(dl)
    b_bar = ((lam_bar - 1.0) / lam)[..., None] * lax.complex(b_re, b_im)
    c = lax.complex(c_re, c_im)
    steps = jnp.arange(ch + 1, dtype=a_re.dtype)
    pw = jnp.exp(dl[:, None, :] * steps[None, :, None])
    kd = jnp.einsum('gop,gdp,gpi->gdoi', c, pw[:, :ch], b_bar).real
    lag = jnp.arange(ch)[None, :] - jnp.arange(ch)[:, None]
    kfull = kd[:, jnp.clip(lag, 0, ch - 1)]
    kfull = jnp.where((lag >= 0)[None, :, :, None, None], kfull, 0.0)
    k_tab = kfull.transpose(0, 1, 4, 2, 3).reshape(g, ch * h, ch * h)
    bc = pw[:, ch - 1 - jnp.arange(ch)][:, :, :, None] * b_bar[:, None]
    bc = bc.transpose(0, 1, 3, 2).reshape(g, ch * h, p)
    b_tab = jnp.concatenate([bc.real, bc.imag], axis=-1)
    m = c[:, None] * pw[:, 1:ch + 1][:, :, None, :]
    m = m.transpose(0, 3, 1, 2).reshape(g, p, ch * h)
    c_tab = jnp.concatenate([m.real, -m.imag], axis=1)
    pows = (ch * (2 ** jnp.arange(n_scan))).astype(a_re.dtype)
    ak = jnp.exp(dl[:, None, :] * pows[None, :, None])
    p1 = jnp.concatenate([ak.real, ak.real], axis=-1)
    p2 = jnp.concatenate([-ak.imag, ak.imag], axis=-1)
    return k_tab, b_tab, c_tab, p1, p2


def _s5_kernel(n_scan, rows_per_seq, u_ref, k_ref, b_ref, c_ref, p1_ref, p2_ref, y_ref):
    u = u_ref[0]
    state = _dot(u, b_ref[0])
    row = lax.broadcasted_iota(jnp.int32, state.shape, 0) & (rows_per_seq - 1)
    half = state.shape[1] // 2
    for k in range(n_scan):
        shift = 1 << k
        prev = jnp.where(row >= shift, pltpu.roll(state, shift, axis=0), 0.0)
        state = state + prev * p1_ref[0, k:k + 1, :] + pltpu.roll(prev, half, axis=1) * p2_ref[0, k:k + 1, :]
    carried = jnp.where(row >= 1, pltpu.roll(state, 1, axis=0), 0.0)
    y_ref[0] = _dot(u, k_ref[0]) + _dot(carried.astype(BF16), c_ref[0])


def _s5_conv(u, tables, batch, seqlen):
    k_tab, b_tab, c_tab, p1, p2 = tables
    t, width = u.shape
    g = width // S5_GROUP
    ch = S5_CHUNK
    rows = t // ch
    rows_per_seq = seqlen // ch
    n_scan = p1.shape[1]
    ug = u.reshape(rows, ch, g, S5_GROUP).transpose(2, 0, 1, 3).reshape(g, rows, ch * S5_GROUP).astype(BF16)
    kw = ch * S5_GROUP
    sw = b_tab.shape[-1]
    y = pl.pallas_call(
        functools.partial(_s5_kernel, n_scan, rows_per_seq),
        grid=(g,),
        in_specs=[pl.BlockSpec((1, rows, kw), lambda i: (i, 0, 0)),
                  pl.BlockSpec((1, kw, kw), lambda i: (i, 0, 0)),
                  pl.BlockSpec((1, kw, sw), lambda i: (i, 0, 0)),
                  pl.BlockSpec((1, sw, kw), lambda i: (i, 0, 0)),
                  pl.BlockSpec((1, n_scan, sw), lambda i: (i, 0, 0)),
                  pl.BlockSpec((1, n_scan, sw), lambda i: (i, 0, 0))],
        out_specs=pl.BlockSpec((1, rows, kw), lambda i: (i, 0, 0)),
        out_shape=jax.ShapeDtypeStruct((g, rows, kw), F32),
        compiler_params=_params("parallel"), name="s5_conv",
    )(ug, k_tab.astype(BF16), b_tab.astype(BF16), c_tab.astype(BF16), p1, p2)
    return y.reshape(g, rows, ch, S5_GROUP).transpose(1, 2, 0, 3).reshape(t, width)


def _gelu_tanh(x):
    return 0.5 * x * (1.0 + jnp.tanh(math.sqrt(2.0 / math.pi) * (x + 0.044715 * (x * x * x))))


def _s5_glu_kernel(y_ref, u_ref, d_ref, w_ref, b_ref, o_ref):
    y = _gelu_tanh(y_ref[...] + d_ref[...] * u_ref[...])
    gate = jax.nn.sigmoid(_dot(y.astype(BF16), w_ref[...]) + b_ref[...])
    o_ref[...] = (y * gate).astype(o_ref.dtype)


def _s5_glu(y, u, d_skip, w_glu, b_glu, tm=1024):
    t, width = y.shape
    row = pl.BlockSpec((tm, width), lambda i: (i, 0))
    vec = pl.BlockSpec((1, width), lambda i: (0, 0))
    return pl.pallas_call(
        _s5_glu_kernel, grid=(t // tm,),
        in_specs=[row, row, vec, pl.BlockSpec((width, width), lambda i: (0, 0)), vec],
        out_specs=row, out_shape=jax.ShapeDtypeStruct((t, width), BF16),
        compiler_params=_params("parallel"), name="s5_glu",
    )(y, u, d_skip.reshape(1, width), w_glu.astype(BF16), b_glu.reshape(1, width))


def _to_qt(t, batch, heads, blk):
    tt, w = t.shape
    d = w // heads
    nb = tt // batch // blk
    return t.reshape(batch, nb, blk, heads, d).transpose(0, 3, 1, 4, 2).reshape(batch * heads, nb, d, blk)


def _to_k(t, batch, heads, blk):
    tt, w = t.shape
    d = w // heads
    nb = tt // batch // blk
    return t.reshape(batch, nb, blk, heads, d).transpose(0, 3, 1, 2, 4).reshape(batch * heads, nb, blk, d)


def _from_ot(o, batch, heads):
    bh, nb, d, blk = o.shape
    return o.reshape(batch, heads, nb, d, blk).transpose(0, 2, 4, 1, 3).reshape(batch * nb * blk, heads * d)


def _diag_valid(bk, bq, d, strict):
    key = lax.broadcasted_iota(jnp.int32, (bk, bq), 0) + d * bk
    qry = lax.broadcasted_iota(jnp.int32, (bk, bq), 1)
    return key < qry if strict else key <= qry


def _softplus(z):
    return jnp.maximum(z, 0.0) + jnp.log(1.0 + jnp.exp(-jnp.abs(z)))


def _sb_kernel(qt_ref, k_ref, vt_ref, tri_ref, ot_ref):
    nq, bq = qt_ref.shape[1], qt_ref.shape[3]
    bk, dv = k_ref.shape[2], vt_ref.shape[2]
    tb = tri_ref.shape[0]
    ratio = bq // bk

    def block(qt, j, carry, acc, diag):
        z = _dot(k_ref[0, j], qt)
        sp = _softplus(z)
        if diag is not None:
            valid = _diag_valid(bk, bq, diag, True)
            sp = jnp.where(valid, sp, 0.0)
        hi = sp.astype(BF16)
        lo = (sp - hi.astype(F32)).astype(BF16)
        parts = []
        for t in reversed(range(bk // tb)):
            rows = slice(t * tb, (t + 1) * tb)
            part = _dot(tri_ref[...], hi[rows]) + _dot(tri_ref[...], lo[rows]) + carry
            carry = part[0:1, :]
            parts.insert(0, part)
        csum = jnp.concatenate(parts, axis=0)
        w = jnp.exp(z - csum)
        if diag is not None:
            w = jnp.where(valid, w, 0.0)
        acc = acc + _dot(vt_ref[0, j], w.astype(BF16))
        return carry, acc

    def q_block(qi, _):
        qt = qt_ref[0, qi]
        carry = jnp.zeros((1, bq), F32)
        acc = jnp.zeros((dv, bq), F32)
        for d in reversed(range(ratio)):
            carry, acc = block(qt, qi * ratio + d, carry, acc, d)

        def k_block(i, c):
            return block(qt, qi * ratio - 1 - i, c[0], c[1], None)

        carry, acc = lax.fori_loop(0, qi * ratio, k_block, (carry, acc))
        ot_ref[0, qi] = acc.astype(ot_ref.dtype)
        return 0

    lax.fori_loop(0, nq, q_block, 0)


def _sb_attention(qt, k, vt):
    bh, nq, dq, bq = qt.shape
    nk, bk = k.shape[1], k.shape[2]
    dv = vt.shape[2]
    tb = MXU_TILE
    tri = (lax.broadcasted_iota(jnp.int32, (tb, tb), 0) <= lax.broadcasted_iota(jnp.int32, (tb, tb), 1)).astype(BF16)
    return pl.pallas_call(
        _sb_kernel, grid=(bh,),
        in_specs=[pl.BlockSpec((1, nq, dq, bq), lambda i: (i, 0, 0, 0)),
                  pl.BlockSpec((1, nk, bk, dq), lambda i: (i, 0, 0, 0)),
                  pl.BlockSpec((1, nk, dv, bk), lambda i: (i, 0, 0, 0)),
                  pl.BlockSpec((tb, tb), lambda i: (0, 0))],
        out_specs=pl.BlockSpec((1, nq, dv, bq), lambda i: (i, 0, 0, 0)),
        out_shape=jax.ShapeDtypeStruct((bh, nq, dv, bq), BF16),
        compiler_params=_params("parallel"), name="sb_attention",
    )(qt, k, vt, tri)


def _fox_kernel(qt_ref, k_ref, vt_ref, bias_ref, ot_ref):
    nq, bq = qt_ref.shape[1], qt_ref.shape[3]
    bk, dv = k_ref.shape[2], vt_ref.shape[2]
    ratio = bq // bk

    def block(qt, j, m, l, acc, diag):
        bias = bias_ref[0, j]
        s = _dot(k_ref[0, j], qt) + jnp.concatenate([bias] * (bq // LANES), axis=1)
        if diag is not None:
            s = jnp.where(_diag_valid(bk, bq, diag, False), s, -jnp.inf)
        m_new = jnp.maximum(m, jnp.max(s, axis=0, keepdims=True))
        alpha = jnp.exp(m - m_new)
        p = jnp.exp(s - m_new)
        l = alpha * l + jnp.sum(p, axis=0, keepdims=True)
        acc = alpha * acc + _dot(vt_ref[0, j], p.astype(BF16))
        return m_new, l, acc

    def q_block(qi, _):
        qt = qt_ref[0, qi]
        m = jnp.full((1, bq), -jnp.inf, F32)
        l = jnp.zeros((1, bq), F32)
        acc = jnp.zeros((dv, bq), F32)

        def k_block(j, c):
            return block(qt, j, c[0], c[1], c[2], None)

        m, l, acc = lax.fori_loop(0, qi * ratio, k_block, (m, l, acc))
        for d in range(ratio):
            m, l, acc = block(qt, qi * ratio + d, m, l, acc, d)
        ot_ref[0, qi] = (acc / l).astype(ot_ref.dtype)
        return 0

    lax.fori_loop(0, nq, q_block, 0)


def _fox_attention(qt, k, vt, bias):
    bh, nq, dq, bq = qt.shape
    nk, bk = k.shape[1], k.shape[2]
    dv = vt.shape[2]
    return pl.pallas_call(
        _fox_kernel, grid=(bh,),
        in_specs=[pl.BlockSpec((1, nq, dq, bq), lambda i: (i, 0, 0, 0)),
                  pl.BlockSpec((1, nk, bk, dq), lambda i: (i, 0, 0, 0)),
                  pl.BlockSpec((1, nk, dv, bk), lambda i: (i, 0, 0, 0)),
                  pl.BlockSpec((1, nk, bk, LANES), lambda i: (i, 0, 0, 0))],
        out_specs=pl.BlockSpec((1, nq, dv, bq), lambda i: (i, 0, 0, 0)),
        out_shape=jax.ShapeDtypeStruct((bh, nq, dv, bq), BF16),
        compiler_params=_params("parallel"), name="fox_attention",
    )(qt, k, vt, bias)


def _split3(x):
    hi = x.astype(BF16)
    r = x - hi.astype(F32)
    mid = r.astype(BF16)
    lo = (r - mid.astype(F32)).astype(BF16)
    return hi, mid, lo


def _fox_gate_kernel(f_ref, b_ref, tri_ref, c_ref, carry_ref):
    @pl.when(pl.program_id(1) == 0)
    def _():
        carry_ref[...] = jnp.zeros_like(carry_ref)

    x = f_ref[...] + b_ref[...]
    log_f = jnp.minimum(x, 0.0) - jnp.log(1.0 + jnp.exp(-jnp.abs(x)))
    hi, mid, lo = _split3(log_f)
    tri = tri_ref[...]
    c = _dot(tri, hi) + _dot(tri, mid) + _dot(tri, lo) + carry_ref[...]
    c_ref[...] = c
    carry_ref[...] = c[c.shape[0] - 1:, :]


def _fox_gate_cumsum(f_logit, bias, batch, seqlen, tc=512):
    t, w = f_logit.shape
    nchunk = seqlen // tc
    tri = (lax.broadcasted_iota(jnp.int32, (tc, tc), 1) <= lax.broadcasted_iota(jnp.int32, (tc, tc), 0)).astype(BF16)
    return pl.pallas_call(
        _fox_gate_kernel, grid=(batch, nchunk),
        in_specs=[pl.BlockSpec((tc, w), lambda b, c: (b * nchunk + c, 0)),
                  pl.BlockSpec((1, w), lambda b, c: (0, 0)),
                  pl.BlockSpec((tc, tc), lambda b, c: (0, 0))],
        out_specs=pl.BlockSpec((tc, w), lambda b, c: (b * nchunk + c, 0)),
        out_shape=jax.ShapeDtypeStruct((t, w), F32),
        scratch_shapes=[pltpu.VMEM((1, w), F32)],
        compiler_params=_params("parallel", "arbitrary"), name="fox_gate_cumsum",
    )(f_logit, bias, tri)


def _router_kernel(x_ref, g_ref, whi_ref, wlo_ref, o_ref):
    h = _rms(x_ref[...], g_ref[...])
    h_hi = h.astype(BF16)
    h_lo = (h - h_hi.astype(F32)).astype(BF16)
    logits = _dot(h_hi, whi_ref[...]) + _dot(h_lo, whi_ref[...]) + _dot(h_hi, wlo_ref[...])
    lane = lax.broadcasted_iota(jnp.int32, logits.shape, 1)
    logits = jnp.where(lane < N_EXPERTS, logits, -jnp.inf)
    m1 = jnp.max(logits, axis=1, keepdims=True)
    i1 = jnp.min(jnp.where(logits == m1, lane, LANES), axis=1, keepdims=True)
    rest = jnp.where(lane == i1, -jnp.inf, logits)
    m2 = jnp.max(rest, axis=1, keepdims=True)
    i2 = jnp.min(jnp.where(rest == m2, lane, LANES), axis=1, keepdims=True)
    e2 = jnp.exp(m2 - m1)
    g1 = 1.0 / (1.0 + e2)
    o_ref[...] = jnp.where(lane == i1, g1, 0.0) + jnp.where(lane == i2, e2 * g1, 0.0)


def _router(x, gain, w_router, tm=1024):
    t, d = x.shape
    w = jnp.zeros((d, LANES), F32).at[:, :N_EXPERTS].set(w_router)
    w_hi = w.astype(BF16)
    w_lo = (w - w_hi.astype(F32)).astype(BF16)
    return pl.pallas_call(
        _router_kernel, grid=(t // tm,),
        in_specs=[pl.BlockSpec((tm, d), lambda i: (i, 0)), pl.BlockSpec((1, d), lambda i: (0, 0)),
                  pl.BlockSpec((d, LANES), lambda i: (0, 0)), pl.BlockSpec((d, LANES), lambda i: (0, 0))],
        out_specs=pl.BlockSpec((tm, LANES), lambda i: (i, 0)),
        out_shape=jax.ShapeDtypeStruct((t, LANES), F32),
        compiler_params=_params("parallel"), name="router",
    )(x, gain.reshape(1, d), w_hi, w_lo)


def _swiglu_kernel(has_comb, has_final, x_ref, g_ref, *refs):
    refs = list(refs)
    comb_ref = refs.pop(0) if has_comb else None
    wg_ref, wu_ref, wd_ref = refs[:3]
    refs = refs[3:]
    fg_ref = refs.pop(0) if has_final else None
    o_ref, h_ref, acc_ref = refs
    e, f = pl.program_id(1), pl.program_id(2)
    first = jnp.logical_and(e == 0, f == 0)
    last = jnp.logical_and(e == pl.num_programs(1) - 1, f == pl.num_programs(2) - 1)

    @pl.when(first)
    def _():
        h_ref[...] = _rms(x_ref[...], g_ref[...]).astype(BF16)
        acc_ref[...] = jnp.zeros_like(acc_ref)

    h = h_ref[...]
    gate = _dot(h, wg_ref[0])
    up = _dot(h, wu_ref[0])
    act = gate * jax.nn.sigmoid(gate) * up
    if has_comb:
        lane = lax.broadcasted_iota(jnp.int32, comb_ref.shape, 1)
        act = act * jnp.sum(jnp.where(lane == e, comb_ref[...], 0.0), axis=1, keepdims=True)
    acc_ref[...] += _dot(act.astype(BF16), wd_ref[0])

    @pl.when(last)
    def _():
        out = x_ref[...] + acc_ref[...]
        if has_final:
            out = _rms(out, fg_ref[...])
        o_ref[...] = out


def _swiglu(x, gain, w_gate, w_up, w_down, comb=None, final_gain=None, tm=1024, tf=256):
    t, d = x.shape
    n_e, _, ff = w_gate.shape
    row = pl.BlockSpec((tm, d), lambda i, e, f: (i, 0))
    vec = pl.BlockSpec((1, d), lambda i, e, f: (0, 0))
    in_specs, args = [row, vec], [x, gain.reshape(1, d)]
    if comb is not None:
        in_specs.append(pl.BlockSpec((tm, LANES), lambda i, e, f: (i, 0)))
        args.append(comb)
    in_specs += [pl.BlockSpec((1, d, tf), lambda i, e, f: (e, 0, f)),
                 pl.BlockSpec((1, d, tf), lambda i, e, f: (e, 0, f)),
                 pl.BlockSpec((1, tf, d), lambda i, e, f: (e, f, 0))]
    args += [w_gate, w_up, w_down]
    if final_gain is not None:
        in_specs.append(vec)
        args.append(final_gain.reshape(1, d))
    return pl.pallas_call(
        functools.partial(_swiglu_kernel, comb is not None, final_gain is not None),
        grid=(t // tm, n_e, ff // tf), in_specs=in_specs, out_specs=row,
        out_shape=jax.ShapeDtypeStruct((t, d), F32),
        scratch_shapes=[pltpu.VMEM((tm, d), BF16), pltpu.VMEM((tm, d), F32)],
        compiler_params=_params("parallel", "arbitrary", "arbitrary"), name="swiglu",
    )(*args)


def _even_layer(x, batch, seqlen, norm_mix, w_in, s5, w_glu, b_glu, d_skip, w_out, norm_ffn, w_gate, w_up, w_down):
    s5_width = d_skip.size
    sb_width = (w_in.shape[1] - s5_width) // 3
    heads = sb_width // SB_HEAD_DIM
    scale = 1.0 / math.sqrt(SB_HEAD_DIM)
    w_u = w_in[:, :s5_width].astype(BF16)
    w_q = (w_in[:, s5_width:s5_width + sb_width] * scale).astype(BF16)
    w_kv = w_in[:, s5_width + sb_width:].astype(BF16)
    u, q, kv = _norm_proj(x, norm_mix, [w_u, w_q, w_kv], [F32, BF16, BF16])

    n_scan = int(math.log2(seqlen // S5_CHUNK))
    tables = _s5_tables(*s5, n_scan)
    y = _s5_conv(u, tables, batch, seqlen)
    a_out = _s5_glu(y, u, d_skip, w_glu, b_glu)

    qt = _to_qt(q, batch, heads, ATT_BQ)
    k = _to_k(kv[:, :sb_width], batch, heads, ATT_BK)
    vt = _to_qt(kv[:, sb_width:], batch, heads, ATT_BK)
    b_out = _from_ot(_sb_attention(qt, k, vt), batch, heads)

    w_out = w_out.astype(BF16)
    x = _proj_residual(x, [a_out, b_out], [w_out[:s5_width], w_out[s5_width:]])
    return _swiglu(x, norm_ffn, w_gate.astype(BF16)[None], w_up.astype(BF16)[None], w_down.astype(BF16)[None], tf=256)


def _odd_layer(x, batch, seqlen, norm_mix, w_in, b_f, w_out, norm_ffn, w_router, w_gate, w_up, w_down, final_gain):
    heads = b_f.size
    width = heads * FOX_HEAD_DIM
    scale = 1.0 / math.sqrt(FOX_HEAD_DIM)
    d = x.shape[1]
    w_q = (w_in[:, :width] * scale).astype(BF16)
    w_kv = w_in[:, width:3 * width].astype(BF16)
    w_f = jnp.zeros((d, LANES), F32).at[:, :heads].set(w_in[:, 3 * width:]).astype(BF16)
    q, kv, f_logit = _norm_proj(x, norm_mix, [w_q, w_kv, w_f], [BF16, BF16, F32])
    bias = jnp.zeros((1, LANES), F32).at[0, :heads].set(b_f)
    cum = _fox_gate_cumsum(f_logit, bias, batch, seqlen)[:, :heads]

    nb = seqlen // ATT_BK
    bias = (-cum).reshape(batch, nb, ATT_BK, heads).transpose(0, 3, 1, 2).reshape(batch * heads, nb, ATT_BK, 1)
    bias = jnp.broadcast_to(bias, (batch * heads, nb, ATT_BK, LANES))

    qt = _to_qt(q, batch, heads, ATT_BQ)
    k = _to_k(kv[:, :width], batch, heads, ATT_BK)
    vt = _to_qt(kv[:, width:], batch, heads, ATT_BK)
    c_out = _from_ot(_fox_attention(qt, k, vt, bias), batch, heads)

    x = _proj_residual(x, [c_out], [w_out.astype(BF16)])
    comb = _router(x, norm_ffn, w_router)
    return _swiglu(x, norm_ffn, w_gate.astype(BF16), w_up.astype(BF16), w_down.astype(BF16),
                   comb=comb, final_gain=final_gain, tf=512)


def kernel(x, ev_norm_mix, ev_w_in, s5_a_re, s5_a_im, s5_b_re, s5_b_im, s5_c_re, s5_c_im, s5_d, s5_log_step,
           s5_w_glu, s5_b_glu, ev_w_out, ev_norm_ffn, ffn_w_gate, ffn_w_up, ffn_w_down, od_norm_mix, od_w_in,
           fox_b_f, od_w_out, od_norm_ffn, moe_w_router, moe_w_gate, moe_w_up, moe_w_down, final_norm):
    batch, seqlen, d = x.shape
    n_even, n_odd = ev_w_in.shape[0], od_w_in.shape[0]
    depth = n_even + n_odd
    h = x.reshape(batch * seqlen, d)
    for layer in range(depth):
        i = layer // 2
        last = layer == depth - 1
        if layer % 2 == 0:
            s5 = (s5_a_re[i], s5_a_im[i], s5_b_re[i], s5_b_im[i], s5_c_re[i], s5_c_im[i], s5_log_step[i])
            h = _even_layer(h, batch, seqlen, ev_norm_mix[i], ev_w_in[i], s5, s5_w_glu[i], s5_b_glu[i],
                            s5_d[i].reshape(-1), ev_w_out[i], ev_norm_ffn[i], ffn_w_gate[i], ffn_w_up[i],
                            ffn_w_down[i])
        else:
            h = _odd_layer(h, batch, seqlen, od_norm_mix[i], od_w_in[i], fox_b_f[i], od_w_out[i], od_norm_ffn[i],
                           moe_w_router[i], moe_w_gate[i], moe_w_up[i], moe_w_down[i],
                           final_norm if last else None)
    return h.reshape(batch, seqlen, d)
```

```python
import functools
import math

import jax
import jax.numpy as jnp
from jax import lax
from jax.experimental import pallas as pl
from jax.experimental.pallas import tpu as pltpu

F32 = jnp.float32
BF16 = jnp.bfloat16

RMS_EPS = 1e-6
LANES = 128
VMEM_LIMIT_BYTES = 56 * 1024 * 1024

S5_GROUP = 16
S5_STATE = 64
S5_CHUNK = 16
SB_HEAD_DIM = 64
FOX_HEAD_DIM = 64
N_EXPERTS = 8
TOP_K = 2
MOE_TILE = 1024
GATHER_ROWS = 1024
COMBINE_ROWS = 512
MXU_TILE = 256
ATT_BQ = 1024
ATT_BK = 512


def _params(*sem):
    return pltpu.CompilerParams(dimension_semantics=sem, vmem_limit_bytes=VMEM_LIMIT_BYTES)


def _rms(x, g):
    ms = jnp.mean(x * x, axis=-1, keepdims=True)
    return x * lax.rsqrt(ms + RMS_EPS) * g


def _dot(a, b):
    return jnp.dot(a, b, preferred_element_type=F32)


def _norm_proj_kernel(n_out, x_ref, g_ref, *refs):
    w_refs, o_refs = refs[:n_out], refs[n_out:]
    h = _rms(x_ref[...], g_ref[...]).astype(BF16)
    for w_ref, o_ref in zip(w_refs, o_refs):
        o_ref[...] = _dot(h, w_ref[...]).astype(o_ref.dtype)


def _norm_proj(x, gain, weights, out_dtypes, tm=512):
    t, d = x.shape
    n_out = len(weights)
    in_specs = [pl.BlockSpec((tm, d), lambda i: (i, 0)), pl.BlockSpec((1, d), lambda i: (0, 0))]
    in_specs += [pl.BlockSpec(w.shape, lambda i: (0, 0)) for w in weights]
    out_specs = [pl.BlockSpec((tm, w.shape[1]), lambda i: (i, 0)) for w in weights]
    out_shape = [jax.ShapeDtypeStruct((t, w.shape[1]), dt) for w, dt in zip(weights, out_dtypes)]
    return pl.pallas_call(
        functools.partial(_norm_proj_kernel, n_out),
        grid=(t // tm,), in_specs=in_specs, out_specs=out_specs, out_shape=out_shape,
        compiler_params=_params("parallel"), name="norm_proj",
    )(x, gain.reshape(1, d), *weights)


def _proj_residual_kernel(n_in, res_ref, *refs):
    a_refs, w_refs, o_ref = refs[:n_in], refs[n_in:2 * n_in], refs[2 * n_in]
    acc = res_ref[...]
    for a_ref, w_ref in zip(a_refs, w_refs):
        acc = acc + _dot(a_ref[...], w_ref[...])
    o_ref[...] = acc


def _proj_residual(res, acts, weights, tm=512):
    t, n = res.shape
    n_in = len(acts)
    in_specs = [pl.BlockSpec((tm, n), lambda i: (i, 0))]
    in_specs += [pl.BlockSpec((tm, a.shape[1]), lambda i: (i, 0)) for a in acts]
    in_specs += [pl.BlockSpec(w.shape, lambda i: (0, 0)) for w in weights]
    return pl.pallas_call(
        functools.partial(_proj_residual_kernel, n_in),
        grid=(t // tm,), in_specs=in_specs, out_specs=pl.BlockSpec((tm, n), lambda i: (i, 0)),
        out_shape=jax.ShapeDtypeStruct((t, n), F32),
        compiler_params=_params("parallel"), name="proj_residual",
    )(res, *acts, *weights)


def _s5_tables(a_re, a_im, b_re, b_im, c_re, c_im, log_step, n_scan):
    g, p = a_re.shape
    h = b_re.shape[-1]
    ch = S5_CHUNK
    lam = lax.complex(a_re, a_im)
    dl = lam * jnp.exp(log_step)[:, None]
    lam_bar = jnp.exp(dl)
    b_bar = ((lam_bar - 1.0) / lam)[..., None] * lax.complex(b_re, b_im)
    c = lax.complex(c_re, c_im)
    steps = jnp.arange(ch + 1, dtype=a_re.dtype)
    pw = jnp.exp(dl[:, None, :] * steps[None, :, None])
    kd = jnp.einsum('gop,gdp,gpi->gdoi', c, pw[:, :ch], b_bar).real
    lag = jnp.arange(ch)[None, :] - jnp.arange(ch)[:, None]
    kfull = kd[:, jnp.clip(lag, 0, ch - 1)]
    kfull = jnp.where((lag >= 0)[None, :, :, None, None], kfull, 0.0)
    k_tab = kfull.transpose(0, 1, 4, 2, 3).reshape(g, ch * h, ch * h)
    bc = pw[:, ch - 1 - jnp.arange(ch)][:, :, :, None] * b_bar[:, None]
    bc = bc.transpose(0, 1, 3, 2).reshape(g, ch * h, p)
    b_tab = jnp.concatenate([bc.real, bc.imag], axis=-1)
    m = c[:, None] * pw[:, 1:ch + 1][:, :, None, :]
    m = m.transpose(0, 3, 1, 2).reshape(g, p, ch * h)
    c_tab = jnp.concatenate([m.real, -m.imag], axis=1)
    pows = (ch * (2 ** jnp.arange(n_scan))).astype(a_re.dtype)
    ak = jnp.exp(dl[:, None, :] * pows[None, :, None])
    p1 = jnp.concatenate([ak.real, ak.real], axis=-1)
    p2 = jnp.concatenate([-ak.imag, ak.imag], axis=-1)
    return k_tab, b_tab, c_tab, p1, p2


def _s5_kernel(n_scan, rows_per_seq, u_ref, k_ref, b_ref, c_ref, p1_ref, p2_ref, y_ref):
    u = u_ref[0]
    state = _dot(u, b_ref[0])
    row = lax.broadcasted_iota(jnp.int32, state.shape, 0) & (rows_per_seq - 1)
    half = state.shape[1] // 2
    for k in range(n_scan):
        shift = 1 << k
        prev = jnp.where(row >= shift, pltpu.roll(state, shift, axis=0), 0.0)
        state = state + prev * p1_ref[0, k:k + 1, :] + pltpu.roll(prev, half, axis=1) * p2_ref[0, k:k + 1, :]
    carried = jnp.where(row >= 1, pltpu.roll(state, 1, axis=0), 0.0)
    y_ref[0] = _dot(u, k_ref[0]) + _dot(carried.astype(BF16), c_ref[0])


def _s5_conv(u, tables, batch, seqlen):
    k_tab, b_tab, c_tab, p1, p2 = tables
    t, width = u.shape
    g = width // S5_GROUP
    ch = S5_CHUNK
    rows = t // ch
    rows_per_seq = seqlen // ch
    n_scan = p1.shape[1]
    ug = u.reshape(rows, ch, g, S5_GROUP).transpose(2, 0, 1, 3).reshape(g, rows, ch * S5_GROUP).astype(BF16)
    kw = ch * S5_GROUP
    sw = b_tab.shape[-1]
    y = pl.pallas_call(
        functools.partial(_s5_kernel, n_scan, rows_per_seq),
        grid=(g,),
        in_specs=[pl.BlockSpec((1, rows, kw), lambda i: (i, 0, 0)),
                  pl.BlockSpec((1, kw, kw), lambda i: (i, 0, 0)),
                  pl.BlockSpec((1, kw, sw), lambda i: (i, 0, 0)),
                  pl.BlockSpec((1, sw, kw), lambda i: (i, 0, 0)),
                  pl.BlockSpec((1, n_scan, sw), lambda i: (i, 0, 0)),
                  pl.BlockSpec((1, n_scan, sw), lambda i: (i, 0, 0))],
        out_specs=pl.BlockSpec((1, rows, kw), lambda i: (i, 0, 0)),
        out_shape=jax.ShapeDtypeStruct((g, rows, kw), F32),
        compiler_params=_params("parallel"), name="s5_conv",
    )(ug, k_tab.astype(BF16), b_tab.astype(BF16), c_tab.astype(BF16), p1, p2)
    return y.reshape(g, rows, ch, S5_GROUP).transpose(1, 2, 0, 3).reshape(t, width)


def _gelu_tanh(x):
    return 0.5 * x * (1.0 + jnp.tanh(math.sqrt(2.0 / math.pi) * (x + 0.044715 * (x * x * x))))


def _s5_glu_kernel(y_ref, u_ref, d_ref, w_ref, b_ref, o_ref):
    y = _gelu_tanh(y_ref[...] + d_ref[...] * u_ref[...])
    gate = jax.nn.sigmoid(_dot(y.astype(BF16), w_ref[...]) + b_ref[...])
    o_ref[...] = (y * gate).astype(o_ref.dtype)


def _s5_glu(y, u, d_skip, w_glu, b_glu, tm=1024):
    t, width = y.shape
    row = pl.BlockSpec((tm, width), lambda i: (i, 0))
    vec = pl.BlockSpec((1, width), lambda i: (0, 0))
    return pl.pallas_call(
        _s5_glu_kernel, grid=(t // tm,),
        in_specs=[row, row, vec, pl.BlockSpec((width, width), lambda i: (0, 0)), vec],
        out_specs=row, out_shape=jax.ShapeDtypeStruct((t, width), BF16),
        compiler_params=_params("parallel"), name="s5_glu",
    )(y, u, d_skip.reshape(1, width), w_glu.astype(BF16), b_glu.reshape(1, width))


def _to_qt(t, batch, heads, blk):
    tt, w = t.shape
    d = w // heads
    nb = tt // batch // blk
    return t.reshape(batch, nb, blk, heads, d).transpose(0, 3, 1, 4, 2).reshape(batch * heads, nb, d, blk)


def _to_k(t, batch, heads, blk):
    tt, w = t.shape
    d = w // heads
    nb = tt // batch // blk
    return t.reshape(batch, nb, blk, heads, d).transpose(0, 3, 1, 2, 4).reshape(batch * heads, nb, blk, d)


def _from_ot(o, batch, heads):
    bh, nb, d, blk = o.shape
    return o.reshape(batch, heads, nb, d, blk).transpose(0, 2, 4, 1, 3).reshape(batch * nb * blk, heads * d)


def _diag_valid(bk, bq, d, strict):
    key = lax.broadcasted_iota(jnp.int32, (bk, bq), 0) + d * bk
    qry = lax.broadcasted_iota(jnp.int32, (bk, bq), 1)
    return key < qry if strict else key <= qry


def _softplus(z):
    return jnp.maximum(z, 0.0) + jnp.log(1.0 + jnp.exp(-jnp.abs(z)))


def _sb_kernel(qt_ref, k_ref, vt_ref, tri_ref, ot_ref):
    nq, bq = qt_ref.shape[1], qt_ref.shape[3]
    bk, dv = k_ref.shape[2], vt_ref.shape[2]
    tb = tri_ref.shape[0]
    ratio = bq // bk

    def block(qt, j, carry, acc, diag):
        z = _dot(k_ref[0, j], qt)
        sp = _softplus(z)
        if diag is not None:
            valid = _diag_valid(bk, bq, diag, True)
            sp = jnp.where(valid, sp, 0.0)
        hi = sp.astype(BF16)
        lo = (sp - hi.astype(F32)).astype(BF16)
        parts = []
        for t in reversed(range(bk // tb)):
            rows = slice(t * tb, (t + 1) * tb)
            part = _dot(tri_ref[...], hi[rows]) + _dot(tri_ref[...], lo[rows]) + carry
            carry = part[0:1, :]
            parts.insert(0, part)
        csum = jnp.concatenate(parts, axis=0)
        w = jnp.exp(z - csum)
        if diag is not None:
            w = jnp.where(valid, w, 0.0)
        acc = acc + _dot(vt_ref[0, j], w.astype(BF16))
        return carry, acc

    def q_block(qi, _):
        qt = qt_ref[0, qi]
        carry = jnp.zeros((1, bq), F32)
        acc = jnp.zeros((dv, bq), F32)
        for d in reversed(range(ratio)):
            carry, acc = block(qt, qi * ratio + d, carry, acc, d)

        def k_block(i, c):
            return block(qt, qi * ratio - 1 - i, c[0], c[1], None)

        carry, acc = lax.fori_loop(0, qi * ratio, k_block, (carry, acc))
        ot_ref[0, qi] = acc.astype(ot_ref.dtype)
        return 0

    lax.fori_loop(0, nq, q_block, 0)


def _sb_attention(qt, k, vt):
    bh, nq, dq, bq = qt.shape
    nk, bk = k.shape[1], k.shape[2]
    dv = vt.shape[2]
    tb = MXU_TILE
    tri = (lax.broadcasted_iota(jnp.int32, (tb, tb), 0) <= lax.broadcasted_iota(jnp.int32, (tb, tb), 1)).astype(BF16)
    return pl.pallas_call(
        _sb_kernel, grid=(bh,),
        in_specs=[pl.BlockSpec((1, nq, dq, bq), lambda i: (i, 0, 0, 0)),
                  pl.BlockSpec((1, nk, bk, dq), lambda i: (i, 0, 0, 0)),
                  pl.BlockSpec((1, nk, dv, bk), lambda i: (i, 0, 0, 0)),
                  pl.BlockSpec((tb, tb), lambda i: (0, 0))],
        out_specs=pl.BlockSpec((1, nq, dv, bq), lambda i: (i, 0, 0, 0)),
        out_shape=jax.ShapeDtypeStruct((bh, nq, dv, bq), BF16),
        compiler_params=_params("parallel"), name="sb_attention",
    )(qt, k, vt, tri)


def _fox_kernel(qt_ref, k_ref, vt_ref, bias_ref, ot_ref):
    nq, bq = qt_ref.shape[1], qt_ref.shape[3]
    bk, dv = k_ref.shape[2], vt_ref.shape[2]
    ratio = bq // bk

    def block(qt, j, m, l, acc, diag):
        bias = bias_ref[0, j]
        s = _dot(k_ref[0, j], qt) + jnp.concatenate([bias] * (bq // LANES), axis=1)
        if diag is not None:
            s = jnp.where(_diag_valid(bk, bq, diag, False), s, -jnp.inf)
        m_new = jnp.maximum(m, jnp.max(s, axis=0, keepdims=True))
        alpha = jnp.exp(m - m_new)
        p = jnp.exp(s - m_new)
        l = alpha * l + jnp.sum(p, axis=0, keepdims=True)
        acc = alpha * acc + _dot(vt_ref[0, j], p.astype(BF16))
        return m_new, l, acc

    def q_block(qi, _):
        qt = qt_ref[0, qi]
        m = jnp.full((1, bq), -jnp.inf, F32)
        l = jnp.zeros((1, bq), F32)
        acc = jnp.zeros((dv, bq), F32)

        def k_block(j, c):
            return block(qt, j, c[0], c[1], c[2], None)

        m, l, acc = lax.fori_loop(0, qi * ratio, k_block, (m, l, acc))
        for d in range(ratio):
            m, l, acc = block(qt, qi * ratio + d, m, l, acc, d)
        ot_ref[0, qi] = (acc / l).astype(ot_ref.dtype)
        return 0

    lax.fori_loop(0, nq, q_block, 0)


def _fox_attention(qt, k, vt, bias):
    bh, nq, dq, bq = qt.shape
    nk, bk = k.shape[1], k.shape[2]
    dv = vt.shape[2]
    return pl.pallas_call(
        _fox_kernel, grid=(bh,),
        in_specs=[pl.BlockSpec((1, nq, dq, bq), lambda i: (i, 0, 0, 0)),
                  pl.BlockSpec((1, nk, bk, dq), lambda i: (i, 0, 0, 0)),
                  pl.BlockSpec((1, nk, dv, bk), lambda i: (i, 0, 0, 0)),
                  pl.BlockSpec((1, nk, bk, LANES), lambda i: (i, 0, 0, 0))],
        out_specs=pl.BlockSpec((1, nq, dv, bq), lambda i: (i, 0, 0, 0)),
        out_shape=jax.ShapeDtypeStruct((bh, nq, dv, bq), BF16),
        compiler_params=_params("parallel"), name="fox_attention",
    )(qt, k, vt, bias)


def _split3(x):
    hi = x.astype(BF16)
    r = x - hi.astype(F32)
    mid = r.astype(BF16)
    lo = (r - mid.astype(F32)).astype(BF16)
    return hi, mid, lo


def _fox_gate_kernel(f_ref, b_ref, tri_ref, c_ref, carry_ref):
    @pl.when(pl.program_id(1) == 0)
    def _():
        carry_ref[...] = jnp.zeros_like(carry_ref)

    x = f_ref[...] + b_ref[...]
    log_f = jnp.minimum(x, 0.0) - jnp.log(1.0 + jnp.exp(-jnp.abs(x)))
    hi, mid, lo = _split3(log_f)
    tri = tri_ref[...]
    c = _dot(tri, hi) + _dot(tri, mid) + _dot(tri, lo) + carry_ref[...]
    c_ref[...] = c
    carry_ref[...] = c[c.shape[0] - 1:, :]


def _fox_gate_cumsum(f_logit, bias, batch, seqlen, tc=512):
    t, w = f_logit.shape
    nchunk = seqlen // tc
    tri = (lax.broadcasted_iota(jnp.int32, (tc, tc), 1) <= lax.broadcasted_iota(jnp.int32, (tc, tc), 0)).astype(BF16)
    return pl.pallas_call(
        _fox_gate_kernel, grid=(batch, nchunk),
        in_specs=[pl.BlockSpec((tc, w), lambda b, c: (b * nchunk + c, 0)),
                  pl.BlockSpec((1, w), lambda b, c: (0, 0)),
                  pl.BlockSpec((tc, tc), lambda b, c: (0, 0))],
        out_specs=pl.BlockSpec((tc, w), lambda b, c: (b * nchunk + c, 0)),
        out_shape=jax.ShapeDtypeStruct((t, w), F32),
        scratch_shapes=[pltpu.VMEM((1, w), F32)],
        compiler_params=_params("parallel", "arbitrary"), name="fox_gate_cumsum",
    )(f_logit, bias, tri)


def _router_kernel(x_ref, g_ref, whi_ref, wlo_ref, o_ref):
    h = _rms(x_ref[...], g_ref[...])
    h_hi = h.astype(BF16)
    h_lo = (h - h_hi.astype(F32)).astype(BF16)
    logits = _dot(h_hi, whi_ref[...]) + _dot(h_lo, whi_ref[...]) + _dot(h_hi, wlo_ref[...])
    lane = lax.broadcasted_iota(jnp.int32, logits.shape, 1)
    logits = jnp.where(lane < N_EXPERTS, logits, -jnp.inf)
    m1 = jnp.max(logits, axis=1, keepdims=True)
    i1 = jnp.min(jnp.where(logits == m1, lane, LANES), axis=1, keepdims=True)
    rest = jnp.where(lane == i1, -jnp.inf, logits)
    m2 = jnp.max(rest, axis=1, keepdims=True)
    i2 = jnp.min(jnp.where(rest == m2, lane, LANES), axis=1, keepdims=True)
    e2 = jnp.exp(m2 - m1)
    g1 = 1.0 / (1.0 + e2)
    out = jnp.where(lane == 0, g1, 0.0) + jnp.where(lane == 1, e2 * g1, 0.0)
    out = out + jnp.where(lane == 2, i1.astype(F32), 0.0) + jnp.where(lane == 3, i2.astype(F32), 0.0)
    o_ref[...] = out


def _router(x, gain, w_router, tm=1024):
    t, d = x.shape
    w = jnp.zeros((d, LANES), F32).at[:, :N_EXPERTS].set(w_router)
    w_hi = w.astype(BF16)
    w_lo = (w - w_hi.astype(F32)).astype(BF16)
    return pl.pallas_call(
        _router_kernel, grid=(t // tm,),
        in_specs=[pl.BlockSpec((tm, d), lambda i: (i, 0)), pl.BlockSpec((1, d), lambda i: (0, 0)),
                  pl.BlockSpec((d, LANES), lambda i: (0, 0)), pl.BlockSpec((d, LANES), lambda i: (0, 0))],
        out_specs=pl.BlockSpec((tm, LANES), lambda i: (i, 0)),
        out_shape=jax.ShapeDtypeStruct((t, LANES), F32),
        compiler_params=_params("parallel"), name="router",
    )(x, gain.reshape(1, d), w_hi, w_lo)


def _route_plan(route, tm):
    t = route.shape[0]
    n_pad = TOP_K * t + N_EXPERTS * tm
    n_tiles = n_pad // tm
    expert = route[:, 2:2 + TOP_K].astype(jnp.int32).T.reshape(-1)
    gates = route[:, 0:TOP_K].T.reshape(-1)
    onehot = (expert[:, None] == jnp.arange(N_EXPERTS, dtype=jnp.int32)[None, :]).astype(jnp.int32)
    rank = jnp.sum((jnp.cumsum(onehot, axis=0) - onehot) * onehot, axis=1)
    counts = jnp.sum(onehot, axis=0)
    padded = ((counts + tm - 1) // tm) * tm
    ends = jnp.cumsum(padded)
    pos = (ends - padded)[expert] + rank
    token = jnp.tile(jnp.arange(t, dtype=jnp.int32), TOP_K)
    src = jnp.zeros((n_pad,), jnp.int32).at[pos].set(token)
    gate = jnp.zeros((n_pad,), F32).at[pos].set(gates).reshape(n_pad, 1)
    tile_start = jnp.arange(n_tiles, dtype=jnp.int32) * tm
    tile_expert = jnp.minimum(jnp.sum((tile_start[:, None] >= ends[None, :]).astype(jnp.int32), axis=1), N_EXPERTS - 1)
    n_used = (ends[-1] // tm).astype(jnp.int32).reshape(1)
    return src, gate, pos.reshape(TOP_K, t), tile_expert.astype(jnp.int32), n_used


def _gather_rows_kernel(rows, idx_hbm, x_hbm, o_hbm, idx_smem, idx_sem, row_sem):
    i = pl.program_id(0)
    idx_copy = pltpu.make_async_copy(idx_hbm.at[i], idx_smem, idx_sem)
    idx_copy.start()
    idx_copy.wait()
    base = i * rows

    def issue(r, _):
        pltpu.make_async_copy(x_hbm.at[pl.ds(idx_smem[r], 1)], o_hbm.at[pl.ds(base + r, 1)], row_sem).start()
        return 0

    lax.fori_loop(0, rows, issue, 0, unroll=8)
    pltpu.make_async_copy(o_hbm.at[pl.ds(base, rows)], o_hbm.at[pl.ds(base, rows)], row_sem).wait()


def _gather_rows(x, src, rows=GATHER_ROWS):
    n, d = src.shape[0], x.shape[1]
    return pl.pallas_call(
        functools.partial(_gather_rows_kernel, rows), grid=(n // rows,),
        in_specs=[pl.BlockSpec(memory_space=pl.ANY), pl.BlockSpec(memory_space=pl.ANY)],
        out_specs=pl.BlockSpec(memory_space=pl.ANY),
        out_shape=jax.ShapeDtypeStruct((n, d), x.dtype),
        scratch_shapes=[pltpu.SMEM((rows,), jnp.int32), pltpu.SemaphoreType.DMA(()), pltpu.SemaphoreType.DMA(())],
        compiler_params=_params("arbitrary"), name="gather_rows",
    )(src.reshape(n // rows, rows), x)


def _experts_kernel(te_ref, nu_ref, x_ref, g_ref, gate_ref, wg_ref, wu_ref, wd_ref, o_ref, h_ref, acc_ref):
    i, f = pl.program_id(0), pl.program_id(1)
    last = f == pl.num_programs(1) - 1
    used = i < nu_ref[0]

    @pl.when(jnp.logical_and(used, f == 0))
    def _():
        h_ref[...] = _rms(x_ref[...], g_ref[...]).astype(BF16)
        acc_ref[...] = jnp.zeros_like(acc_ref)

    @pl.when(used)
    def _():
        h = h_ref[...]
        gate = _dot(h, wg_ref[0])
        up = _dot(h, wu_ref[0])
        act = gate * jax.nn.sigmoid(gate) * up
        acc_ref[...] += _dot(act.astype(BF16), wd_ref[0])

    @pl.when(jnp.logical_and(used, last))
    def _():
        o_ref[...] = acc_ref[...] * gate_ref[...]

    @pl.when(jnp.logical_and(jnp.logical_not(used), last))
    def _():
        o_ref[...] = jnp.zeros_like(o_ref)


def _experts(xs, gain, gate, tile_expert, n_used, w_gate, w_up, w_down, tm, tf=512):
    n, d = xs.shape
    ff = w_gate.shape[2]
    nf = ff // tf

    def wcol(i, f, te, nu):
        return te[i], 0, jnp.where(i < nu[0], f, nf - 1)

    def wrow(i, f, te, nu):
        return te[i], jnp.where(i < nu[0], f, nf - 1), 0

    def row(i, f, te, nu):
        return i, 0

    grid_spec = pltpu.PrefetchScalarGridSpec(
        num_scalar_prefetch=2, grid=(n // tm, nf),
        in_specs=[pl.BlockSpec((tm, d), row), pl.BlockSpec((1, d), lambda i, f, te, nu: (0, 0)),
                  pl.BlockSpec((tm, 1), row),
                  pl.BlockSpec((1, d, tf), wcol), pl.BlockSpec((1, d, tf), wcol), pl.BlockSpec((1, tf, d), wrow)],
        out_specs=pl.BlockSpec((tm, d), row),
        scratch_shapes=[pltpu.VMEM((tm, d), BF16), pltpu.VMEM((tm, d), F32)])
    return pl.pallas_call(
        _experts_kernel, grid_spec=grid_spec, out_shape=jax.ShapeDtypeStruct((n, d), F32),
        compiler_params=_params("arbitrary", "arbitrary"), name="experts",
    )(tile_expert, n_used, xs, gain.reshape(1, d), gate, w_gate, w_up, w_down)


def _combine_kernel(rows, idx_hbm, x_ref, g_ref, y_hbm, o_ref, idx_smem, buf, idx_sem, row_sem):
    i = pl.program_id(0)
    idx_copy = pltpu.make_async_copy(idx_hbm.at[i], idx_smem, idx_sem)
    idx_copy.start()
    idx_copy.wait()

    def issue(r, _):
        for k in range(TOP_K):
            pltpu.make_async_copy(y_hbm.at[pl.ds(idx_smem[k * rows + r], 1)], buf.at[k, pl.ds(r, 1)], row_sem).start()
        return 0

    lax.fori_loop(0, rows, issue, 0, unroll=8)
    pltpu.make_async_copy(buf, buf, row_sem).wait()
    out = x_ref[...]
    for k in range(TOP_K):
        out = out + buf[k]
    o_ref[...] = _rms(out, g_ref[...])


def _combine(x, y, pos, gain, rows=COMBINE_ROWS):
    t, d = x.shape
    idx = pos.reshape(TOP_K, t // rows, rows).transpose(1, 0, 2).reshape(t // rows, TOP_K * rows)
    return pl.pallas_call(
        functools.partial(_combine_kernel, rows), grid=(t // rows,),
        in_specs=[pl.BlockSpec(memory_space=pl.ANY), pl.BlockSpec((rows, d), lambda i: (i, 0)),
                  pl.BlockSpec((1, d), lambda i: (0, 0)), pl.BlockSpec(memory_space=pl.ANY)],
        out_specs=pl.BlockSpec((rows, d), lambda i: (i, 0)),
        out_shape=jax.ShapeDtypeStruct((t, d), F32),
        scratch_shapes=[pltpu.SMEM((TOP_K * rows,), jnp.int32), pltpu.VMEM((TOP_K, rows, d), F32),
                        pltpu.SemaphoreType.DMA(()), pltpu.SemaphoreType.DMA(())],
        compiler_params=_params("arbitrary"), name="moe_combine",
    )(idx, x, gain.reshape(1, d), y)


def _swiglu_kernel(x_ref, g_ref, wg_ref, wu_ref, wd_ref, o_ref, h_ref, acc_ref):
    f = pl.program_id(1)

    @pl.when(f == 0)
    def _():
        h_ref[...] = _rms(x_ref[...], g_ref[...]).astype(BF16)
        acc_ref[...] = jnp.zeros_like(acc_ref)

    h = h_ref[...]
    gate = _dot(h, wg_ref[...])
    up = _dot(h, wu_ref[...])
    act = gate * jax.nn.sigmoid(gate) * up
    acc_ref[...] += _dot(act.astype(BF16), wd_ref[...])

    @pl.when(f == pl.num_programs(1) - 1)
    def _():
        o_ref[...] = x_ref[...] + acc_ref[...]


def _swiglu(x, gain, w_gate, w_up, w_down, tm=1024, tf=256):
    t, d = x.shape
    ff = w_gate.shape[1]
    row = pl.BlockSpec((tm, d), lambda i, f: (i, 0))
    return pl.pallas_call(
        _swiglu_kernel, grid=(t // tm, ff // tf),
        in_specs=[row, pl.BlockSpec((1, d), lambda i, f: (0, 0)),
                  pl.BlockSpec((d, tf), lambda i, f: (0, f)), pl.BlockSpec((d, tf), lambda i, f: (0, f)),
                  pl.BlockSpec((tf, d), lambda i, f: (f, 0))],
        out_specs=row, out_shape=jax.ShapeDtypeStruct((t, d), F32),
        scratch_shapes=[pltpu.VMEM((tm, d), BF16), pltpu.VMEM((tm, d), F32)],
        compiler_params=_params("parallel", "arbitrary"), name="swiglu",
    )(x, gain.reshape(1, d), w_gate, w_up, w_down)


def _even_layer(x, batch, seqlen, norm_mix, w_in, s5, w_glu, b_glu, d_skip, w_out, norm_ffn, w_gate, w_up, w_down):
    s5_width = d_skip.size
    sb_width = (w_in.shape[1] - s5_width) // 3
    heads = sb_width // SB_HEAD_DIM
    scale = 1.0 / math.sqrt(SB_HEAD_DIM)
    w_u = w_in[:, :s5_width].astype(BF16)
    w_q = (w_in[:, s5_width:s5_width + sb_width] * scale).astype(BF16)
    w_kv = w_in[:, s5_width + sb_width:].astype(BF16)
    u, q, kv = _norm_proj(x, norm_mix, [w_u, w_q, w_kv], [F32, BF16, BF16])

    n_scan = int(math.log2(seqlen // S5_CHUNK))
    tables = _s5_tables(*s5, n_scan)
    y = _s5_conv(u, tables, batch, seqlen)
    a_out = _s5_glu(y, u, d_skip, w_glu, b_glu)

    qt = _to_qt(q, batch, heads, ATT_BQ)
    k = _to_k(kv[:, :sb_width], batch, heads, ATT_BK)
    vt = _to_qt(kv[:, sb_width:], batch, heads, ATT_BK)
    b_out = _from_ot(_sb_attention(qt, k, vt), batch, heads)

    w_out = w_out.astype(BF16)
    x = _proj_residual(x, [a_out, b_out], [w_out[:s5_width], w_out[s5_width:]])
    return _swiglu(x, norm_ffn, w_gate.astype(BF16), w_up.astype(BF16), w_down.astype(BF16))


def _odd_layer(x, batch, seqlen, norm_mix, w_in, b_f, w_out, norm_ffn, w_router, w_gate, w_up, w_down, final_gain):
    heads = b_f.size
    width = heads * FOX_HEAD_DIM
    scale = 1.0 / math.sqrt(FOX_HEAD_DIM)
    d = x.shape[1]
    w_q = (w_in[:, :width] * scale).astype(BF16)
    w_kv = w_in[:, width:3 * width].astype(BF16)
    w_f = jnp.zeros((d, LANES), F32).at[:, :heads].set(w_in[:, 3 * width:]).astype(BF16)
    q, kv, f_logit = _norm_proj(x, norm_mix, [w_q, w_kv, w_f], [BF16, BF16, F32])
    bias = jnp.zeros((1, LANES), F32).at[0, :heads].set(b_f)
    cum = _fox_gate_cumsum(f_logit, bias, batch, seqlen)[:, :heads]

    nb = seqlen // ATT_BK
    bias = (-cum).reshape(batch, nb, ATT_BK, heads).transpose(0, 3, 1, 2).reshape(batch * heads, nb, ATT_BK, 1)
    bias = jnp.broadcast_to(bias, (batch * heads, nb, ATT_BK, LANES))

    qt = _to_qt(q, batch, heads, ATT_BQ)
    k = _to_k(kv[:, :width], batch, heads, ATT_BK)
    vt = _to_qt(kv[:, width:], batch, heads, ATT_BK)
    c_out = _from_ot(_fox_attention(qt, k, vt, bias), batch, heads)

    x = _proj_residual(x, [c_out], [w_out.astype(BF16)])
    route = _router(x, norm_ffn, w_router)
    src, gate, pos, tile_expert, n_used = _route_plan(route, MOE_TILE)
    xs = _gather_rows(x, src)
    ys = _experts(xs, norm_ffn, gate, tile_expert, n_used, w_gate.astype(BF16), w_up.astype(BF16),
                  w_down.astype(BF16), MOE_TILE)
    return _combine(x, ys, pos, final_gain)


def kernel(x, ev_norm_mix, ev_w_in, s5_a_re, s5_a_im, s5_b_re, s5_b_im, s5_c_re, s5_c_im, s5_d, s5_log_step,
           s5_w_glu, s5_b_glu, ev_w_out, ev_norm_ffn, ffn_w_gate, ffn_w_up, ffn_w_down, od_norm_mix, od_w_in,
           fox_b_f, od_w_out, od_norm_ffn, moe_w_router, moe_w_gate, moe_w_up, moe_w_down, final_norm):
    batch, seqlen, d = x.shape
    assert ev_w_in.shape[0] == 1 and od_w_in.shape[0] == 1, "one even and one odd layer"
    h = x.reshape(batch * seqlen, d)
    s5 = (s5_a_re[0], s5_a_im[0], s5_b_re[0], s5_b_im[0], s5_c_re[0], s5_c_im[0], s5_log_step[0])
    h = _even_layer(h, batch, seqlen, ev_norm_mix[0], ev_w_in[0], s5, s5_w_glu[0], s5_b_glu[0], s5_d[0].reshape(-1),
                    ev_w_out[0], ev_norm_ffn[0], ffn_w_gate[0], ffn_w_up[0], ffn_w_down[0])
    h = _odd_layer(h, batch, seqlen, od_norm_mix[0], od_w_in[0], fox_b_f[0], od_w_out[0], od_norm_ffn[0],
                   moe_w_router[0], moe_w_gate[0], moe_w_up[0], moe_w_down[0], final_norm)
    return h.reshape(batch, seqlen, d)
```

```python
import functools
import math

import jax
import jax.numpy as jnp
from jax import lax
from jax.experimental import pallas as pl
from jax.experimental.pallas import tpu as pltpu

F32 = jnp.float32
BF16 = jnp.bfloat16

RMS_EPS = 1e-6
LANES = 128
VMEM_LIMIT_BYTES = 56 * 1024 * 1024

S5_GROUP = 16
S5_STATE = 64
S5_CHUNK = 16
SB_HEAD_DIM = 64
FOX_HEAD_DIM = 64
N_EXPERTS = 8
TOP_K = 2
MOE_TILE = 1024
DISPATCH_ROWS = 512
COMBINE_ROWS = 512
MXU_TILE = 256
ATT_BQ = 1024
ATT_BK = 512


def _params(*sem):
    return pltpu.CompilerParams(dimension_semantics=sem, vmem_limit_bytes=VMEM_LIMIT_BYTES)


def _rms(x, g):
    ms = jnp.mean(x * x, axis=-1, keepdims=True)
    return x * lax.rsqrt(ms + RMS_EPS) * g


def _dot(a, b):
    return jnp.dot(a, b, preferred_element_type=F32)


def _norm_proj_kernel(n_out, x_ref, g_ref, *refs):
    w_refs, o_refs = refs[:n_out], refs[n_out:]
    h = _rms(x_ref[...], g_ref[...]).astype(BF16)
    for w_ref, o_ref in zip(w_refs, o_refs):
        o_ref[...] = _dot(h, w_ref[...]).astype(o_ref.dtype)


def _norm_proj(x, gain, weights, out_dtypes, tm=512):
    t, d = x.shape
    n_out = len(weights)
    in_specs = [pl.BlockSpec((tm, d), lambda i: (i, 0)), pl.BlockSpec((1, d), lambda i: (0, 0))]
    in_specs += [pl.BlockSpec(w.shape, lambda i: (0, 0)) for w in weights]
    out_specs = [pl.BlockSpec((tm, w.shape[1]), lambda i: (i, 0)) for w in weights]
    out_shape = [jax.ShapeDtypeStruct((t, w.shape[1]), dt) for w, dt in zip(weights, out_dtypes)]
    return pl.pallas_call(
        functools.partial(_norm_proj_kernel, n_out),
        grid=(t // tm,), in_specs=in_specs, out_specs=out_specs, out_shape=out_shape,
        compiler_params=_params("parallel"), name="norm_proj",
    )(x, gain.reshape(1, d), *weights)


def _proj_residual_kernel(n_in, res_ref, *refs):
    a_refs, w_refs, o_ref = refs[:n_in], refs[n_in:2 * n_in], refs[2 * n_in]
    acc = res_ref[...]
    for a_ref, w_ref in zip(a_refs, w_refs):
        acc = acc + _dot(a_ref[...], w_ref[...])
    o_ref[...] = acc


def _proj_residual(res, acts, weights, tm=512):
    t, n = res.shape
    n_in = len(acts)
    in_specs = [pl.BlockSpec((tm, n), lambda i: (i, 0))]
    in_specs += [pl.BlockSpec((tm, a.shape[1]), lambda i: (i, 0)) for a in acts]
    in_specs += [pl.BlockSpec(w.shape, lambda i: (0, 0)) for w in weights]
    return pl.pallas_call(
        functools.partial(_proj_residual_kernel, n_in),
        grid=(t // tm,), in_specs=in_specs, out_specs=pl.BlockSpec((tm, n), lambda i: (i, 0)),
        out_shape=jax.ShapeDtypeStruct((t, n), F32),
        compiler_params=_params("parallel"), name="proj_residual",
    )(res, *acts, *weights)


def _s5_tables(a_re, a_im, b_re, b_im, c_re, c_im, log_step, n_scan):
    g, p = a_re.shape
    h = b_re.shape[-1]
    ch = S5_CHUNK
    lam = lax.complex(a_re, a_im)
    dl = lam * jnp.exp(log_step)[:, None]
    lam_bar = jnp.exp(dl)
    b_bar = ((lam_bar - 1.0) / lam)[..., None] * lax.complex(b_re, b_im)
    c = lax.complex(c_re, c_im)
    steps = jnp.arange(ch + 1, dtype=a_re.dtype)
    pw = jnp.exp(dl[:, None, :] * steps[None, :, None])
    kd = jnp.einsum('gop,gdp,gpi->gdoi', c, pw[:, :ch], b_bar).real
    lag = jnp.arange(ch)[None, :] - jnp.arange(ch)[:, None]
    kfull = kd[:, jnp.clip(lag, 0, ch - 1)]
    kfull = jnp.where((lag >= 0)[None, :, :, None, None], kfull, 0.0)
    k_tab = kfull.transpose(0, 1, 4, 2, 3).reshape(g, ch * h, ch * h)
    bc = pw[:, ch - 1 - jnp.arange(ch)][:, :, :, None] * b_bar[:, None]
    bc = bc.transpose(0, 1, 3, 2).reshape(g, ch * h, p)
    b_tab = jnp.concatenate([bc.real, bc.imag], axis=-1)
    m = c[:, None] * pw[:, 1:ch + 1][:, :, None, :]
    m = m.transpose(0, 3, 1, 2).reshape(g, p, ch * h)
    c_tab = jnp.concatenate([m.real, -m.imag], axis=1)
    pows = (ch * (2 ** jnp.arange(n_scan))).astype(a_re.dtype)
    ak = jnp.exp(dl[:, None, :] * pows[None, :, None])
    p1 = jnp.concatenate([ak.real, ak.real], axis=-1)
    p2 = jnp.concatenate([-ak.imag, ak.imag], axis=-1)
    return k_tab, b_tab, c_tab, p1, p2


def _s5_kernel(n_scan, rows_per_seq, u_ref, k_ref, b_ref, c_ref, p1_ref, p2_ref, y_ref):
    u = u_ref[0]
    state = _dot(u, b_ref[0])
    row = lax.broadcasted_iota(jnp.int32, state.shape, 0) & (rows_per_seq - 1)
    half = state.shape[1] // 2
    for k in range(n_scan):
        shift = 1 << k
        prev = jnp.where(row >= shift, pltpu.roll(state, shift, axis=0), 0.0)
        state = state + prev * p1_ref[0, k:k + 1, :] + pltpu.roll(prev, half, axis=1) * p2_ref[0, k:k + 1, :]
    carried = jnp.where(row >= 1, pltpu.roll(state, 1, axis=0), 0.0)
    y_ref[0] = _dot(u, k_ref[0]) + _dot(carried.astype(BF16), c_ref[0])


def _s5_conv(u, tables, batch, seqlen):
    k_tab, b_tab, c_tab, p1, p2 = tables
    t, width = u.shape
    g = width // S5_GROUP
    ch = S5_CHUNK
    rows = t // ch
    rows_per_seq = seqlen // ch
    n_scan = p1.shape[1]
    ug = u.reshape(rows, ch, g, S5_GROUP).transpose(2, 0, 1, 3).reshape(g, rows, ch * S5_GROUP).astype(BF16)
    kw = ch * S5_GROUP
    sw = b_tab.shape[-1]
    y = pl.pallas_call(
        functools.partial(_s5_kernel, n_scan, rows_per_seq),
        grid=(g,),
        in_specs=[pl.BlockSpec((1, rows, kw), lambda i: (i, 0, 0)),
                  pl.BlockSpec((1, kw, kw), lambda i: (i, 0, 0)),
                  pl.BlockSpec((1, kw, sw), lambda i: (i, 0, 0)),
                  pl.BlockSpec((1, sw, kw), lambda i: (i, 0, 0)),
                  pl.BlockSpec((1, n_scan, sw), lambda i: (i, 0, 0)),
                  pl.BlockSpec((1, n_scan, sw), lambda i: (i, 0, 0))],
        out_specs=pl.BlockSpec((1, rows, kw), lambda i: (i, 0, 0)),
        out_shape=jax.ShapeDtypeStruct((g, rows, kw), F32),
        compiler_params=_params("parallel"), name="s5_conv",
    )(ug, k_tab.astype(BF16), b_tab.astype(BF16), c_tab.astype(BF16), p1, p2)
    return y.reshape(g, rows, ch, S5_GROUP).transpose(1, 2, 0, 3).reshape(t, width)


def _gelu_tanh(x):
    return 0.5 * x * (1.0 + jnp.tanh(math.sqrt(2.0 / math.pi) * (x + 0.044715 * (x * x * x))))


def _s5_glu_kernel(y_ref, u_ref, d_ref, w_ref, b_ref, o_ref):
    y = _gelu_tanh(y_ref[...] + d_ref[...] * u_ref[...])
    gate = jax.nn.sigmoid(_dot(y.astype(BF16), w_ref[...]) + b_ref[...])
    o_ref[...] = (y * gate).astype(o_ref.dtype)


def _s5_glu(y, u, d_skip, w_glu, b_glu, tm=1024):
    t, width = y.shape
    row = pl.BlockSpec((tm, width), lambda i: (i, 0))
    vec = pl.BlockSpec((1, width), lambda i: (0, 0))
    return pl.pallas_call(
        _s5_glu_kernel, grid=(t // tm,),
        in_specs=[row, row, vec, pl.BlockSpec((width, width), lambda i: (0, 0)), vec],
        out_specs=row, out_shape=jax.ShapeDtypeStruct((t, width), BF16),
        compiler_params=_params("parallel"), name="s5_glu",
    )(y, u, d_skip.reshape(1, width), w_glu.astype(BF16), b_glu.reshape(1, width))


def _to_qt(t, batch, heads, blk):
    tt, w = t.shape
    d = w // heads
    nb = tt // batch // blk
    return t.reshape(batch, nb, blk, heads, d).transpose(0, 3, 1, 4, 2).reshape(batch * heads, nb, d, blk)


def _to_k(t, batch, heads, blk):
    tt, w = t.shape
    d = w // heads
    nb = tt // batch // blk
    return t.reshape(batch, nb, blk, heads, d).transpose(0, 3, 1, 2, 4).reshape(batch * heads, nb, blk, d)


def _from_ot(o, batch, heads):
    bh, nb, d, blk = o.shape
    return o.reshape(batch, heads, nb, d, blk).transpose(0, 2, 4, 1, 3).reshape(batch * nb * blk, heads * d)


def _diag_valid(bk, bq, d, strict):
    key = lax.broadcasted_iota(jnp.int32, (bk, bq), 0) + d * bk
    qry = lax.broadcasted_iota(jnp.int32, (bk, bq), 1)
    return key < qry if strict else key <= qry


def _softplus(z):
    return jnp.maximum(z, 0.0) + jnp.log(1.0 + jnp.exp(-jnp.abs(z)))


def _sb_kernel(qt_ref, k_ref, vt_ref, tri_ref, ot_ref):
    nq, bq = qt_ref.shape[1], qt_ref.shape[3]
    bk, dv = k_ref.shape[2], vt_ref.shape[2]
    tb = tri_ref.shape[0]
    ratio = bq // bk

    def block(qt, j, carry, acc, diag):
        z = _dot(k_ref[0, j], qt)
        sp = _softplus(z)
        if diag is not None:
            valid = _diag_valid(bk, bq, diag, True)
            sp = jnp.where(valid, sp, 0.0)
        hi = sp.astype(BF16)
        lo = (sp - hi.astype(F32)).astype(BF16)
        parts = []
        for t in reversed(range(bk // tb)):
            rows = slice(t * tb, (t + 1) * tb)
            part = _dot(tri_ref[...], hi[rows]) + _dot(tri_ref[...], lo[rows]) + carry
            carry = part[0:1, :]
            parts.insert(0, part)
        csum = jnp.concatenate(parts, axis=0)
        w = jnp.exp(z - csum)
        if diag is not None:
            w = jnp.where(valid, w, 0.0)
        acc = acc + _dot(vt_ref[0, j], w.astype(BF16))
        return carry, acc

    def q_block(qi, _):
        qt = qt_ref[0, qi]
        carry = jnp.zeros((1, bq), F32)
        acc = jnp.zeros((dv, bq), F32)
        for d in reversed(range(ratio)):
            carry, acc = block(qt, qi * ratio + d, carry, acc, d)

        def k_block(i, c):
            return block(qt, qi * ratio - 1 - i, c[0], c[1], None)

        carry, acc = lax.fori_loop(0, qi * ratio, k_block, (carry, acc))
        ot_ref[0, qi] = acc.astype(ot_ref.dtype)
        return 0

    lax.fori_loop(0, nq, q_block, 0)


def _sb_attention(qt, k, vt):
    bh, nq, dq, bq = qt.shape
    nk, bk = k.shape[1], k.shape[2]
    dv = vt.shape[2]
    tb = MXU_TILE
    tri = (lax.broadcasted_iota(jnp.int32, (tb, tb), 0) <= lax.broadcasted_iota(jnp.int32, (tb, tb), 1)).astype(BF16)
    return pl.pallas_call(
        _sb_kernel, grid=(bh,),
        in_specs=[pl.BlockSpec((1, nq, dq, bq), lambda i: (i, 0, 0, 0)),
                  pl.BlockSpec((1, nk, bk, dq), lambda i: (i, 0, 0, 0)),
                  pl.BlockSpec((1, nk, dv, bk), lambda i: (i, 0, 0, 0)),
                  pl.BlockSpec((tb, tb), lambda i: (0, 0))],
        out_specs=pl.BlockSpec((1, nq, dv, bq), lambda i: (i, 0, 0, 0)),
        out_shape=jax.ShapeDtypeStruct((bh, nq, dv, bq), BF16),
        compiler_params=_params("parallel"), name="sb_attention",
    )(qt, k, vt, tri)


def _fox_kernel(qt_ref, k_ref, vt_ref, bias_ref, ot_ref):
    nq, bq = qt_ref.shape[1], qt_ref.shape[3]
    bk, dv = k_ref.shape[2], vt_ref.shape[2]
    ratio = bq // bk

    def block(qt, j, m, l, acc, diag):
        bias = bias_ref[0, j]
        s = _dot(k_ref[0, j], qt) + jnp.concatenate([bias] * (bq // LANES), axis=1)
        if diag is not None:
            s = jnp.where(_diag_valid(bk, bq, diag, False), s, -jnp.inf)
        m_new = jnp.maximum(m, jnp.max(s, axis=0, keepdims=True))
        alpha = jnp.exp(m - m_new)
        p = jnp.exp(s - m_new)
        l = alpha * l + jnp.sum(p, axis=0, keepdims=True)
        acc = alpha * acc + _dot(vt_ref[0, j], p.astype(BF16))
        return m_new, l, acc

    def q_block(qi, _):
        qt = qt_ref[0, qi]
        m = jnp.full((1, bq), -jnp.inf, F32)
        l = jnp.zeros((1, bq), F32)
        acc = jnp.zeros((dv, bq), F32)

        def k_block(j, c):
            return block(qt, j, c[0], c[1], c[2], None)

        m, l, acc = lax.fori_loop(0, qi * ratio, k_block, (m, l, acc))
        for d in range(ratio):
            m, l, acc = block(qt, qi * ratio + d, m, l, acc, d)
        ot_ref[0, qi] = (acc / l).astype(ot_ref.dtype)
        return 0

    lax.fori_loop(0, nq, q_block, 0)


def _fox_attention(qt, k, vt, bias):
    bh, nq, dq, bq = qt.shape
    nk, bk = k.shape[1], k.shape[2]
    dv = vt.shape[2]
    return pl.pallas_call(
        _fox_kernel, grid=(bh,),
        in_specs=[pl.BlockSpec((1, nq, dq, bq), lambda i: (i, 0, 0, 0)),
                  pl.BlockSpec((1, nk, bk, dq), lambda i: (i, 0, 0, 0)),
                  pl.BlockSpec((1, nk, dv, bk), lambda i: (i, 0, 0, 0)),
                  pl.BlockSpec((1, nk, bk, LANES), lambda i: (i, 0, 0, 0))],
        out_specs=pl.BlockSpec((1, nq, dv, bq), lambda i: (i, 0, 0, 0)),
        out_shape=jax.ShapeDtypeStruct((bh, nq, dv, bq), BF16),
        compiler_params=_params("parallel"), name="fox_attention",
    )(qt, k, vt, bias)


def _split3(x):
    hi = x.astype(BF16)
    r = x - hi.astype(F32)
    mid = r.astype(BF16)
    lo = (r - mid.astype(F32)).astype(BF16)
    return hi, mid, lo


def _fox_gate_kernel(f_ref, b_ref, tri_ref, c_ref, carry_ref):
    @pl.when(pl.program_id(1) == 0)
    def _():
        carry_ref[...] = jnp.zeros_like(carry_ref)

    x = f_ref[...] + b_ref[...]
    log_f = jnp.minimum(x, 0.0) - jnp.log(1.0 + jnp.exp(-jnp.abs(x)))
    hi, mid, lo = _split3(log_f)
    tri = tri_ref[...]
    c = _dot(tri, hi) + _dot(tri, mid) + _dot(tri, lo) + carry_ref[...]
    c_ref[...] = c
    carry_ref[...] = c[c.shape[0] - 1:, :]


def _fox_gate_cumsum(f_logit, bias, batch, seqlen, tc=512):
    t, w = f_logit.shape
    nchunk = seqlen // tc
    tri = (lax.broadcasted_iota(jnp.int32, (tc, tc), 1) <= lax.broadcasted_iota(jnp.int32, (tc, tc), 0)).astype(BF16)
    return pl.pallas_call(
        _fox_gate_kernel, grid=(batch, nchunk),
        in_specs=[pl.BlockSpec((tc, w), lambda b, c: (b * nchunk + c, 0)),
                  pl.BlockSpec((1, w), lambda b, c: (0, 0)),
                  pl.BlockSpec((tc, tc), lambda b, c: (0, 0))],
        out_specs=pl.BlockSpec((tc, w), lambda b, c: (b * nchunk + c, 0)),
        out_shape=jax.ShapeDtypeStruct((t, w), F32),
        scratch_shapes=[pltpu.VMEM((1, w), F32)],
        compiler_params=_params("parallel", "arbitrary"), name="fox_gate_cumsum",
    )(f_logit, bias, tri)


def _router_kernel(x_ref, g_ref, whi_ref, wlo_ref, o_ref):
    h = _rms(x_ref[...], g_ref[...])
    h_hi = h.astype(BF16)
    h_lo = (h - h_hi.astype(F32)).astype(BF16)
    logits = _dot(h_hi, whi_ref[...]) + _dot(h_lo, whi_ref[...]) + _dot(h_hi, wlo_ref[...])
    lane = lax.broadcasted_iota(jnp.int32, logits.shape, 1)
    logits = jnp.where(lane < N_EXPERTS, logits, -jnp.inf)
    m1 = jnp.max(logits, axis=1, keepdims=True)
    i1 = jnp.min(jnp.where(logits == m1, lane, LANES), axis=1, keepdims=True)
    rest = jnp.where(lane == i1, -jnp.inf, logits)
    m2 = jnp.max(rest, axis=1, keepdims=True)
    i2 = jnp.min(jnp.where(rest == m2, lane, LANES), axis=1, keepdims=True)
    e2 = jnp.exp(m2 - m1)
    g1 = 1.0 / (1.0 + e2)
    out = jnp.where(lane == 0, g1, 0.0) + jnp.where(lane == 1, e2 * g1, 0.0)
    out = out + jnp.where(lane == 2, i1.astype(F32), 0.0) + jnp.where(lane == 3, i2.astype(F32), 0.0)
    o_ref[...] = out


def _router(x, gain, w_router, tm=1024):
    t, d = x.shape
    w = jnp.zeros((d, LANES), F32).at[:, :N_EXPERTS].set(w_router)
    w_hi = w.astype(BF16)
    w_lo = (w - w_hi.astype(F32)).astype(BF16)
    return pl.pallas_call(
        _router_kernel, grid=(t // tm,),
        in_specs=[pl.BlockSpec((tm, d), lambda i: (i, 0)), pl.BlockSpec((1, d), lambda i: (0, 0)),
                  pl.BlockSpec((d, LANES), lambda i: (0, 0)), pl.BlockSpec((d, LANES), lambda i: (0, 0))],
        out_specs=pl.BlockSpec((tm, LANES), lambda i: (i, 0)),
        out_shape=jax.ShapeDtypeStruct((t, LANES), F32),
        compiler_params=_params("parallel"), name="router",
    )(x, gain.reshape(1, d), w_hi, w_lo)


def _route_plan(route, tm):
    t = route.shape[0]
    n = TOP_K * t
    n_tiles = n // tm
    expert = route[:, 2:2 + TOP_K].astype(jnp.int32).T.reshape(-1)
    onehot = (expert[:, None] == jnp.arange(N_EXPERTS, dtype=jnp.int32)[None, :]).astype(jnp.int32)
    rank = jnp.sum((jnp.cumsum(onehot, axis=0) - onehot) * onehot, axis=1)
    counts = jnp.sum(onehot, axis=0)
    ends = jnp.cumsum(counts)
    pos = (ends - counts)[expert] + rank
    tile_start = jnp.arange(n_tiles, dtype=jnp.int32) * tm
    lo = jnp.sort(jnp.concatenate([tile_start, ends[:-1].astype(jnp.int32)]))
    hi = jnp.concatenate([lo[1:], jnp.full((1,), n, jnp.int32)])
    tile = jnp.minimum(lo // tm, n_tiles - 1)
    exp_id = jnp.minimum(jnp.sum((lo[:, None] >= ends[None, :]).astype(jnp.int32), axis=1), N_EXPERTS - 1)
    first = jnp.concatenate([jnp.ones((1,), jnp.int32), (tile[1:] != tile[:-1]).astype(jnp.int32)])
    last = jnp.concatenate([(tile[1:] != tile[:-1]).astype(jnp.int32), jnp.ones((1,), jnp.int32)])
    visits = jnp.stack([tile, exp_id, lo, hi, first + 2 * last]).astype(jnp.int32)
    return pos.reshape(TOP_K, t).astype(jnp.int32), visits


def _slot_indices(pos, rows):
    t = pos.shape[1]
    return pos.reshape(TOP_K, t // rows, rows).transpose(1, 0, 2).reshape(t // rows, TOP_K * rows)


def _dispatch_kernel(rows, idx_hbm, x_ref, xs_hbm, idx_smem, idx_sem, row_sem):
    idx_copy = pltpu.make_async_copy(idx_hbm.at[pl.program_id(0)], idx_smem, idx_sem)
    idx_copy.start()
    idx_copy.wait()

    def issue(r, _):
        for k in range(TOP_K):
            pltpu.make_async_copy(x_ref.at[pl.ds(r, 1)], xs_hbm.at[pl.ds(idx_smem[k * rows + r], 1)], row_sem).start()
        return 0

    lax.fori_loop(0, rows, issue, 0, unroll=8)
    done = xs_hbm.at[pl.ds(0, TOP_K * rows)]
    pltpu.make_async_copy(done, done, row_sem).wait()


def _dispatch(x, pos, rows=DISPATCH_ROWS):
    t, d = x.shape
    return pl.pallas_call(
        functools.partial(_dispatch_kernel, rows), grid=(t // rows,),
        in_specs=[pl.BlockSpec(memory_space=pl.ANY), pl.BlockSpec((rows, d), lambda i: (i, 0))],
        out_specs=pl.BlockSpec(memory_space=pl.ANY),
        out_shape=jax.ShapeDtypeStruct((TOP_K * t, d), x.dtype),
        scratch_shapes=[pltpu.SMEM((TOP_K * rows,), jnp.int32), pltpu.SemaphoreType.DMA(()),
                        pltpu.SemaphoreType.DMA(())],
        compiler_params=_params("arbitrary"), name="moe_dispatch",
    )(_slot_indices(pos, rows), x)


def _experts_kernel(v_ref, x_ref, g_ref, wg_ref, wu_ref, wd_ref, o_ref, h_ref, acc_ref):
    v, f = pl.program_id(0), pl.program_id(1)
    tm = x_ref.shape[0]
    lo, hi, flags = v_ref[2, v], v_ref[3, v], v_ref[4, v]

    @pl.when(jnp.logical_and((flags & 1) == 1, f == 0))
    def _():
        h_ref[...] = _rms(x_ref[...], g_ref[...]).astype(BF16)
        acc_ref[...] = jnp.zeros_like(acc_ref)

    @pl.when(hi > lo)
    def _():
        h = h_ref[...]
        gate = _dot(h, wg_ref[0])
        up = _dot(h, wu_ref[0])
        row = lax.broadcasted_iota(jnp.int32, (tm, 1), 0) + v_ref[0, v] * tm
        mine = jnp.logical_and(row >= lo, row < hi)
        act = jnp.where(mine, gate * jax.nn.sigmoid(gate) * up, 0.0)
        acc_ref[...] += _dot(act.astype(BF16), wd_ref[0])

    @pl.when(jnp.logical_and((flags & 2) == 2, f == pl.num_programs(1) - 1))
    def _():
        o_ref[...] = acc_ref[...]


def _experts(xs, gain, visits, w_gate, w_up, w_down, tm, tf=512):
    n, d = xs.shape
    ff = w_gate.shape[2]

    def row(v, f, vis):
        return vis[0, v], 0

    grid_spec = pltpu.PrefetchScalarGridSpec(
        num_scalar_prefetch=1, grid=(visits.shape[1], ff // tf),
        in_specs=[pl.BlockSpec((tm, d), row), pl.BlockSpec((1, d), lambda v, f, vis: (0, 0)),
                  pl.BlockSpec((1, d, tf), lambda v, f, vis: (vis[1, v], 0, f)),
                  pl.BlockSpec((1, d, tf), lambda v, f, vis: (vis[1, v], 0, f)),
                  pl.BlockSpec((1, tf, d), lambda v, f, vis: (vis[1, v], f, 0))],
        out_specs=pl.BlockSpec((tm, d), row),
        scratch_shapes=[pltpu.VMEM((tm, d), BF16), pltpu.VMEM((tm, d), F32)])
    return pl.pallas_call(
        _experts_kernel, grid_spec=grid_spec, out_shape=jax.ShapeDtypeStruct((n, d), F32),
        compiler_params=_params("arbitrary", "arbitrary"), name="experts",
    )(visits, xs, gain.reshape(1, d), w_gate, w_up, w_down)


def _combine_kernel(rows, idx_hbm, x_ref, route_ref, g_ref, y_hbm, o_ref, idx_smem, buf, idx_sem, row_sem):
    idx_copy = pltpu.make_async_copy(idx_hbm.at[pl.program_id(0)], idx_smem, idx_sem)
    idx_copy.start()
    idx_copy.wait()

    def issue(r, _):
        for k in range(TOP_K):
            pltpu.make_async_copy(y_hbm.at[pl.ds(idx_smem[k * rows + r], 1)], buf.at[k, pl.ds(r, 1)], row_sem).start()
        return 0

    lax.fori_loop(0, rows, issue, 0, unroll=8)
    pltpu.make_async_copy(buf, buf, row_sem).wait()
    out = x_ref[...]
    for k in range(TOP_K):
        out = out + route_ref[:, k:k + 1] * buf[k]
    o_ref[...] = _rms(out, g_ref[...])


def _combine(x, y, pos, route, gain, rows=COMBINE_ROWS):
    t, d = x.shape
    return pl.pallas_call(
        functools.partial(_combine_kernel, rows), grid=(t // rows,),
        in_specs=[pl.BlockSpec(memory_space=pl.ANY), pl.BlockSpec((rows, d), lambda i: (i, 0)),
                  pl.BlockSpec((rows, LANES), lambda i: (i, 0)), pl.BlockSpec((1, d), lambda i: (0, 0)),
                  pl.BlockSpec(memory_space=pl.ANY)],
        out_specs=pl.BlockSpec((rows, d), lambda i: (i, 0)),
        out_shape=jax.ShapeDtypeStruct((t, d), F32),
        scratch_shapes=[pltpu.SMEM((TOP_K * rows,), jnp.int32), pltpu.VMEM((TOP_K, rows, d), F32),
                        pltpu.SemaphoreType.DMA(()), pltpu.SemaphoreType.DMA(())],
        compiler_params=_params("arbitrary"), name="moe_combine",
    )(_slot_indices(pos, rows), x, route, gain.reshape(1, d), y)


def _swiglu_kernel(x_ref, g_ref, wg_ref, wu_ref, wd_ref, o_ref, h_ref, acc_ref):
    f = pl.program_id(1)

    @pl.when(f == 0)
    def _():
        h_ref[...] = _rms(x_ref[...], g_ref[...]).astype(BF16)
        acc_ref[...] = jnp.zeros_like(acc_ref)

    h = h_ref[...]
    gate = _dot(h, wg_ref[...])
    up = _dot(h, wu_ref[...])
    act = gate * jax.nn.sigmoid(gate) * up
    acc_ref[...] += _dot(act.astype(BF16), wd_ref[...])

    @pl.when(f == pl.num_programs(1) - 1)
    def _():
        o_ref[...] = x_ref[...] + acc_ref[...]


def _swiglu(x, gain, w_gate, w_up, w_down, tm=1024, tf=256):
    t, d = x.shape
    ff = w_gate.shape[1]
    row = pl.BlockSpec((tm, d), lambda i, f: (i, 0))
    return pl.pallas_call(
        _swiglu_kernel, grid=(t // tm, ff // tf),
        in_specs=[row, pl.BlockSpec((1, d), lambda i, f: (0, 0)),
                  pl.BlockSpec((d, tf), lambda i, f: (0, f)), pl.BlockSpec((d, tf), lambda i, f: (0, f)),
                  pl.BlockSpec((tf, d), lambda i, f: (f, 0))],
        out_specs=row, out_shape=jax.ShapeDtypeStruct((t, d), F32),
        scratch_shapes=[pltpu.VMEM((tm, d), BF16), pltpu.VMEM((tm, d), F32)],
        compiler_params=_params("parallel", "arbitrary"), name="swiglu",
    )(x, gain.reshape(1, d), w_gate, w_up, w_down)


def _even_layer(x, batch, seqlen, norm_mix, w_in, s5, w_glu, b_glu, d_skip, w_out, norm_ffn, w_gate, w_up, w_down):
    s5_width = d_skip.size
    sb_width = (w_in.shape[1] - s5_width) // 3
    heads = sb_width // SB_HEAD_DIM
    scale = 1.0 / math.sqrt(SB_HEAD_DIM)
    w_u = w_in[:, :s5_width].astype(BF16)
    w_q = (w_in[:, s5_width:s5_width + sb_width] * scale).astype(BF16)
    w_kv = w_in[:, s5_width + sb_width:].astype(BF16)
    u, q, kv = _norm_proj(x, norm_mix, [w_u, w_q, w_kv], [F32, BF16, BF16])

    n_scan = int(math.log2(seqlen // S5_CHUNK))
    tables = _s5_tables(*s5, n_scan)
    y = _s5_conv(u, tables, batch, seqlen)
    a_out = _s5_glu(y, u, d_skip, w_glu, b_glu)

    qt = _to_qt(q, batch, heads, ATT_BQ)
    k = _to_k(kv[:, :sb_width], batch, heads, ATT_BK)
    vt = _to_qt(kv[:, sb_width:], batch, heads, ATT_BK)
    b_out = _from_ot(_sb_attention(qt, k, vt), batch, heads)

    w_out = w_out.astype(BF16)
    x = _proj_residual(x, [a_out, b_out], [w_out[:s5_width], w_out[s5_width:]])
    return _swiglu(x, norm_ffn, w_gate.astype(BF16), w_up.astype(BF16), w_down.astype(BF16))


def _odd_layer(x, batch, seqlen, norm_mix, w_in, b_f, w_out, norm_ffn, w_router, w_gate, w_up, w_down, final_gain):
    heads = b_f.size
    width = heads * FOX_HEAD_DIM
    scale = 1.0 / math.sqrt(FOX_HEAD_DIM)
    d = x.shape[1]
    w_q = (w_in[:, :width] * scale).astype(BF16)
    w_kv = w_in[:, width:3 * width].astype(BF16)
    w_f = jnp.zeros((d, LANES), F32).at[:, :heads].set(w_in[:, 3 * width:]).astype(BF16)
    q, kv, f_logit = _norm_proj(x, norm_mix, [w_q, w_kv, w_f], [BF16, BF16, F32])
    bias = jnp.zeros((1, LANES), F32).at[0, :heads].set(b_f)
    cum = _fox_gate_cumsum(f_logit, bias, batch, seqlen)[:, :heads]

    nb = seqlen // ATT_BK
    bias = (-cum).reshape(batch, nb, ATT_BK, heads).transpose(0, 3, 1, 2).reshape(batch * heads, nb, ATT_BK, 1)
    bias = jnp.broadcast_to(bias, (batch * heads, nb, ATT_BK, LANES))

    qt = _to_qt(q, batch, heads, ATT_BQ)
    k = _to_k(kv[:, :width], batch, heads, ATT_BK)
    vt = _to_qt(kv[:, width:], batch, heads, ATT_BK)
    c_out = _from_ot(_fox_attention(qt, k, vt, bias), batch, heads)

    x = _proj_residual(x, [c_out], [w_out.astype(BF16)])
    route = _router(x, norm_ffn, w_router)
    pos, visits = _route_plan(route, MOE_TILE)
    xs = _dispatch(x, pos)
    ys = _experts(xs, norm_ffn, visits, w_gate.astype(BF16), w_up.astype(BF16), w_down.astype(BF16), MOE_TILE)
    return _combine(x, ys, pos, route, final_gain)


def kernel(x, ev_norm_mix, ev_w_in, s5_a_re, s5_a_im, s5_b_re, s5_b_im, s5_c_re, s5_c_im, s5_d, s5_log_step,
           s5_w_glu, s5_b_glu, ev_w_out, ev_norm_ffn, ffn_w_gate, ffn_w_up, ffn_w_down, od_norm_mix, od_w_in,
           fox_b_f, od_w_out, od_norm_ffn, moe_w_router, moe_w_gate, moe_w_up, moe_w_down, final_norm):
    batch, seqlen, d = x.shape
    assert ev_w_in.shape[0] == 1 and od_w_in.shape[0] == 1, "one even and one odd layer"
    h = x.reshape(batch * seqlen, d)
    s5 = (s5_a_re[0], s5_a_im[0], s5_b_re[0], s5_b_im[0], s5_c_re[0], s5_c_im[0], s5_log_step[0])
    h = _even_layer(h, batch, seqlen, ev_norm_mix[0], ev_w_in[0], s5, s5_w_glu[0], s5_b_glu[0], s5_d[0].reshape(-1),
                    ev_w_out[0], ev_norm_ffn[0], ffn_w_gate[0], ffn_w_up[0], ffn_w_down[0])
    h = _odd_layer(h, batch, seqlen, od_norm_mix[0], od_w_in[0], fox_b_f[0], od_w_out[0], od_norm_ffn[0],
                   moe_w_router[0], moe_w_gate[0], moe_w_up[0], moe_w_down[0], final_norm)
    return h.reshape(batch, seqlen, d)
```

```python
import functools
import math

import jax
import jax.numpy as jnp
from jax import lax
from jax.experimental import pallas as pl
from jax.experimental.pallas import tpu as pltpu

F32 = jnp.float32
BF16 = jnp.bfloat16

RMS_EPS = 1e-6
LANES = 128
VMEM_LIMIT_BYTES = 56 * 1024 * 1024

S5_GROUP = 16
S5_STATE = 64
S5_CHUNK = 16
SB_HEAD_DIM = 64
FOX_HEAD_DIM = 64
N_EXPERTS = 8
TOP_K = 2
MOE_TILE = 1024
DISPATCH_ROWS = 512
COMBINE_ROWS = 512
MXU_TILE = 256
ATT_BQ = 1024
ATT_BK = 512
LOG2_E = math.log2(math.e)


def _params(*sem):
    return pltpu.CompilerParams(dimension_semantics=sem, vmem_limit_bytes=VMEM_LIMIT_BYTES)


def _rms(x, g):
    ms = jnp.mean(x * x, axis=-1, keepdims=True)
    return x * lax.rsqrt(ms + RMS_EPS) * g


def _dot(a, b):
    return jnp.dot(a, b, preferred_element_type=F32)


_CONTRACT_LAST = (((1,), (1,)), ((), ()))
_CONTRACT_FIRST = (((0,), (0,)), ((), ()))


def _norm_proj_kernel(feature_major, x_ref, g_ref, *refs):
    n_out = len(feature_major)
    w_refs, o_refs = refs[:n_out], refs[n_out:]
    h = _rms(x_ref[...], g_ref[...]).astype(BF16)
    for fm, w_ref, o_ref in zip(feature_major, w_refs, o_refs):
        if fm:
            o_ref[0] = lax.dot_general(w_ref[...], h, _CONTRACT_LAST, preferred_element_type=F32).astype(o_ref.dtype)
        else:
            o_ref[...] = _dot(h, w_ref[...]).astype(o_ref.dtype)


def _norm_proj(x, gain, weights, out_dtypes, feature_major, batch, tm=512):
    t, d = x.shape
    per_seq = t // batch // tm
    in_specs = [pl.BlockSpec((tm, d), lambda i: (i, 0)), pl.BlockSpec((1, d), lambda i: (0, 0))]
    in_specs += [pl.BlockSpec(w.shape, lambda i: (0, 0)) for w in weights]
    out_specs, out_shape = [], []
    for w, dt, fm in zip(weights, out_dtypes, feature_major):
        if fm:
            out_specs.append(pl.BlockSpec((1, w.shape[0], tm), lambda i: (i // per_seq, 0, i % per_seq)))
            out_shape.append(jax.ShapeDtypeStruct((batch, w.shape[0], t // batch), dt))
        else:
            out_specs.append(pl.BlockSpec((tm, w.shape[1]), lambda i: (i, 0)))
            out_shape.append(jax.ShapeDtypeStruct((t, w.shape[1]), dt))
    return pl.pallas_call(
        functools.partial(_norm_proj_kernel, tuple(feature_major)),
        grid=(t // tm,), in_specs=in_specs, out_specs=out_specs, out_shape=out_shape,
        compiler_params=_params("parallel"), name="norm_proj",
    )(x, gain.reshape(1, d), *weights)


def _proj_residual_kernel(feature_major, res_ref, *refs):
    n_in = len(feature_major)
    a_refs, w_refs, o_ref = refs[:n_in], refs[n_in:2 * n_in], refs[2 * n_in]
    acc = res_ref[...]
    for fm, a_ref, w_ref in zip(feature_major, a_refs, w_refs):
        if fm:
            acc = acc + lax.dot_general(a_ref[0], w_ref[...], _CONTRACT_FIRST, preferred_element_type=F32)
        else:
            acc = acc + _dot(a_ref[...], w_ref[...])
    o_ref[...] = acc


def _proj_residual(res, acts, weights, feature_major, batch, tm=512):
    t, n = res.shape
    per_seq = t // batch // tm
    in_specs = [pl.BlockSpec((tm, n), lambda i: (i, 0))]
    for a, fm in zip(acts, feature_major):
        if fm:
            in_specs.append(pl.BlockSpec((1, a.shape[1], tm), lambda i: (i // per_seq, 0, i % per_seq)))
        else:
            in_specs.append(pl.BlockSpec((tm, a.shape[1]), lambda i: (i, 0)))
    in_specs += [pl.BlockSpec(w.shape, lambda i: (0, 0)) for w in weights]
    return pl.pallas_call(
        functools.partial(_proj_residual_kernel, tuple(feature_major)),
        grid=(t // tm,), in_specs=in_specs, out_specs=pl.BlockSpec((tm, n), lambda i: (i, 0)),
        out_shape=jax.ShapeDtypeStruct((t, n), F32),
        compiler_params=_params("parallel"), name="proj_residual",
    )(res, *acts, *weights)


def _s5_tables(a_re, a_im, b_re, b_im, c_re, c_im, log_step, n_scan):
    g, p = a_re.shape
    h = b_re.shape[-1]
    ch = S5_CHUNK
    lam = lax.complex(a_re, a_im)
    dl = lam * jnp.exp(log_step)[:, None]
    lam_bar = jnp.exp(dl)
    b_bar = ((lam_bar - 1.0) / lam)[..., None] * lax.complex(b_re, b_im)
    c = lax.complex(c_re, c_im)
    steps = jnp.arange(ch + 1, dtype=a_re.dtype)
    pw = jnp.exp(dl[:, None, :] * steps[None, :, None])
    kd = jnp.einsum('gop,gdp,gpi->gdoi', c, pw[:, :ch], b_bar).real
    lag = jnp.arange(ch)[None, :] - jnp.arange(ch)[:, None]
    kfull = kd[:, jnp.clip(lag, 0, ch - 1)]
    kfull = jnp.where((lag >= 0)[None, :, :, None, None], kfull, 0.0)
    k_tab = kfull.transpose(0, 1, 4, 2, 3).reshape(g, ch * h, ch * h)
    bc = pw[:, ch - 1 - jnp.arange(ch)][:, :, :, None] * b_bar[:, None]
    bc = bc.transpose(0, 1, 3, 2).reshape(g, ch * h, p)
    b_tab = jnp.concatenate([bc.real, bc.imag], axis=-1)
    m = c[:, None] * pw[:, 1:ch + 1][:, :, None, :]
    m = m.transpose(0, 3, 1, 2).reshape(g, p, ch * h)
    c_tab = jnp.concatenate([m.real, -m.imag], axis=1)
    pows = (ch * (2 ** jnp.arange(n_scan))).astype(a_re.dtype)
    ak = jnp.exp(dl[:, None, :] * pows[None, :, None])
    p1 = jnp.concatenate([ak.real, ak.real], axis=-1)
    p2 = jnp.concatenate([-ak.imag, ak.imag], axis=-1)
    return k_tab, b_tab, c_tab, p1, p2


def _s5_kernel(n_scan, rows_per_seq, u_ref, k_ref, b_ref, c_ref, p1_ref, p2_ref, y_ref):
    u = u_ref[0]
    state = _dot(u, b_ref[0])
    row = lax.broadcasted_iota(jnp.int32, state.shape, 0) & (rows_per_seq - 1)
    half = state.shape[1] // 2
    for k in range(n_scan):
        shift = 1 << k
        prev = jnp.where(row >= shift, pltpu.roll(state, shift, axis=0), 0.0)
        state = state + prev * p1_ref[0, k:k + 1, :] + pltpu.roll(prev, half, axis=1) * p2_ref[0, k:k + 1, :]
    carried = jnp.where(row >= 1, pltpu.roll(state, 1, axis=0), 0.0)
    y_ref[0] = _dot(u, k_ref[0]) + _dot(carried.astype(BF16), c_ref[0])


def _s5_conv(u, tables, batch, seqlen):
    k_tab, b_tab, c_tab, p1, p2 = tables
    t, width = u.shape
    g = width // S5_GROUP
    ch = S5_CHUNK
    rows = t // ch
    rows_per_seq = seqlen // ch
    n_scan = p1.shape[1]
    ug = u.reshape(rows, ch, g, S5_GROUP).transpose(2, 0, 1, 3).reshape(g, rows, ch * S5_GROUP).astype(BF16)
    kw = ch * S5_GROUP
    sw = b_tab.shape[-1]
    y = pl.pallas_call(
        functools.partial(_s5_kernel, n_scan, rows_per_seq),
        grid=(g,),
        in_specs=[pl.BlockSpec((1, rows, kw), lambda i: (i, 0, 0)),
                  pl.BlockSpec((1, kw, kw), lambda i: (i, 0, 0)),
                  pl.BlockSpec((1, kw, sw), lambda i: (i, 0, 0)),
                  pl.BlockSpec((1, sw, kw), lambda i: (i, 0, 0)),
                  pl.BlockSpec((1, n_scan, sw), lambda i: (i, 0, 0)),
                  pl.BlockSpec((1, n_scan, sw), lambda i: (i, 0, 0))],
        out_specs=pl.BlockSpec((1, rows, kw), lambda i: (i, 0, 0)),
        out_shape=jax.ShapeDtypeStruct((g, rows, kw), F32),
        compiler_params=_params("parallel"), name="s5_conv",
    )(ug, k_tab.astype(BF16), b_tab.astype(BF16), c_tab.astype(BF16), p1, p2)
    return y.reshape(g, rows, ch, S5_GROUP).transpose(1, 2, 0, 3).reshape(t, width)


def _gelu_tanh(x):
    return 0.5 * x * (1.0 + jnp.tanh(math.sqrt(2.0 / math.pi) * (x + 0.044715 * (x * x * x))))


def _s5_glu_kernel(y_ref, u_ref, d_ref, w_ref, b_ref, o_ref):
    y = _gelu_tanh(y_ref[...] + d_ref[...] * u_ref[...])
    gate = jax.nn.sigmoid(_dot(y.astype(BF16), w_ref[...]) + b_ref[...])
    o_ref[...] = (y * gate).astype(o_ref.dtype)


def _s5_glu(y, u, d_skip, w_glu, b_glu, tm=1024):
    t, width = y.shape
    row = pl.BlockSpec((tm, width), lambda i: (i, 0))
    vec = pl.BlockSpec((1, width), lambda i: (0, 0))
    return pl.pallas_call(
        _s5_glu_kernel, grid=(t // tm,),
        in_specs=[row, row, vec, pl.BlockSpec((width, width), lambda i: (0, 0)), vec],
        out_specs=row, out_shape=jax.ShapeDtypeStruct((t, width), BF16),
        compiler_params=_params("parallel"), name="s5_glu",
    )(y, u, d_skip.reshape(1, width), w_glu.astype(BF16), b_glu.reshape(1, width))


def _diag_valid(bk, bq, d, strict):
    key = lax.broadcasted_iota(jnp.int32, (bk, bq), 0) + d * bk
    qry = lax.broadcasted_iota(jnp.int32, (bk, bq), 1)
    return key < qry if strict else key <= qry


def _softplus(z):
    return jnp.maximum(z, 0.0) + jnp.log(1.0 + jnp.exp(-jnp.abs(z)))


def _key_rows(j, bk):
    return pl.ds(pl.multiple_of(j * bk, bk), bk)


def _sb_kernel(bk, qt_ref, k_ref, v_ref, tri_ref, ot_ref):
    qi = pl.program_id(2)
    width, bq = qt_ref.shape[1], qt_ref.shape[2]
    dh = width // 2
    tb = tri_ref.shape[0]
    ratio = bq // bk
    feat = lax.broadcasted_iota(jnp.int32, (width, bq), 0)
    qt2 = qt_ref[0]

    def head(p):
        qt = jnp.where((feat >= p * dh) & (feat < (p + 1) * dh), qt2, jnp.zeros_like(qt2))

        def block(j, carry, acc, diag):
            rows = _key_rows(j, bk)
            z = _dot(k_ref[0, rows, :], qt)
            sp = _softplus(z)
            if diag is not None:
                valid = _diag_valid(bk, bq, diag, True)
                sp = jnp.where(valid, sp, 0.0)
            hi = sp.astype(BF16)
            lo = (sp - hi.astype(F32)).astype(BF16)
            parts = []
            for t in reversed(range(bk // tb)):
                sub = slice(t * tb, (t + 1) * tb)
                part = _dot(tri_ref[...], hi[sub]) + _dot(tri_ref[...], lo[sub]) + carry
                carry = part[0:1, :]
                parts.insert(0, part)
            w = jnp.exp(z - jnp.concatenate(parts, axis=0))
            if diag is not None:
                w = jnp.where(valid, w, 0.0)
            pv = lax.dot_general(v_ref[0, rows, :], w.astype(BF16), _CONTRACT_FIRST, preferred_element_type=F32)
            return carry, acc + pv

        carry = jnp.zeros((1, bq), F32)
        acc = jnp.zeros((width, bq), F32)
        for d in reversed(range(ratio)):
            carry, acc = block(qi * ratio + d, carry, acc, d)
        carry, acc = lax.fori_loop(0, qi * ratio, lambda i, c: block(qi * ratio - 1 - i, c[0], c[1], None),
                                   (carry, acc))
        return acc

    acc0, acc1 = head(0), head(1)
    ot_ref[0] = jnp.where(feat < dh, acc0, acc1).astype(ot_ref.dtype)


def _sb_attention(qt, kv, bq=ATT_BQ, bk=ATT_BK):
    batch, width, seqlen = qt.shape
    pairs = width // LANES
    tb = MXU_TILE
    tri = (lax.broadcasted_iota(jnp.int32, (tb, tb), 0) <= lax.broadcasted_iota(jnp.int32, (tb, tb), 1)).astype(BF16)
    qspec = pl.BlockSpec((1, LANES, bq), lambda b, h, q: (b, h, q))
    return pl.pallas_call(
        functools.partial(_sb_kernel, bk), grid=(batch, pairs, seqlen // bq),
        in_specs=[qspec,
                  pl.BlockSpec((1, seqlen, LANES), lambda b, h, q: (b, 0, h)),
                  pl.BlockSpec((1, seqlen, LANES), lambda b, h, q: (b, 0, pairs + h)),
                  pl.BlockSpec((tb, tb), lambda b, h, q: (0, 0))],
        out_specs=qspec, out_shape=jax.ShapeDtypeStruct(qt.shape, BF16),
        compiler_params=_params("parallel", "parallel", "arbitrary"), name="sb_attention",
    )(qt, kv, kv, tri)


def _fox_kernel(bk, qt_ref, k_ref, v_ref, cum_ref, ot_ref, bias_ref):
    hp, qi = pl.program_id(1), pl.program_id(2)
    width, bq = qt_ref.shape[1], qt_ref.shape[2]
    dh = width // 2
    ratio = bq // bk
    feat = lax.broadcasted_iota(jnp.int32, (width, bq), 0)
    vlane = lax.broadcasted_iota(jnp.int32, (bk, width), 1)
    qt2 = qt_ref[0]

    @pl.when(qi == 0)
    def _():
        c = cum_ref[0]
        head_lane = lax.broadcasted_iota(jnp.int32, c.shape, 1)
        for p in range(2):
            col = jnp.sum(jnp.where(head_lane == 2 * hp + p, c, 0.0), axis=1, keepdims=True)
            bias_ref[p] = jnp.broadcast_to(-LOG2_E * col, c.shape)

    def head(p):
        mine = (feat >= p * dh) & (feat < (p + 1) * dh)
        qt = jnp.where(mine, qt2, jnp.zeros_like(qt2))
        vmine = (vlane >= p * dh) & (vlane < (p + 1) * dh)

        def block(j, m, acc, diag):
            rows = _key_rows(j, bk)
            bias = bias_ref[p, rows, :]
            s = _dot(k_ref[0, rows, :], qt) + jnp.concatenate([bias] * (bq // LANES), axis=1)
            if diag is not None:
                s = jnp.where(_diag_valid(bk, bq, diag, False), s, -jnp.inf)
            m_new = jnp.maximum(m, jnp.max(s, axis=0, keepdims=True))
            prob = jnp.exp2(s - m_new).astype(BF16)
            v = v_ref[0, rows, :]
            v = jnp.where(vmine, v, jnp.ones_like(v))
            pv = lax.dot_general(v, prob, _CONTRACT_FIRST, preferred_element_type=F32)
            return m_new, jnp.exp2(m - m_new) * acc + pv

        m = jnp.full((1, bq), -jnp.inf, F32)
        acc = jnp.zeros((width, bq), F32)
        m, acc = lax.fori_loop(0, qi * ratio, lambda j, c: block(j, c[0], c[1], None), (m, acc))
        for d in range(ratio):
            m, acc = block(qi * ratio + d, m, acc, d)
        other = (1 - p) * dh
        return acc / acc[other:other + 1]

    out0, out1 = head(0), head(1)
    ot_ref[0] = jnp.where(feat < dh, out0, out1).astype(ot_ref.dtype)


def _fox_attention(qt, kv, cum, bq=ATT_BQ, bk=ATT_BK):
    batch, width, seqlen = qt.shape
    pairs = width // LANES
    qspec = pl.BlockSpec((1, LANES, bq), lambda b, h, q: (b, h, q))
    return pl.pallas_call(
        functools.partial(_fox_kernel, bk), grid=(batch, pairs, seqlen // bq),
        in_specs=[qspec,
                  pl.BlockSpec((1, seqlen, LANES), lambda b, h, q: (b, 0, h)),
                  pl.BlockSpec((1, seqlen, LANES), lambda b, h, q: (b, 0, pairs + h)),
                  pl.BlockSpec((1, seqlen, LANES), lambda b, h, q: (b, 0, 0))],
        out_specs=qspec, out_shape=jax.ShapeDtypeStruct(qt.shape, BF16),
        scratch_shapes=[pltpu.VMEM((2, seqlen, LANES), F32)],
        compiler_params=_params("parallel", "parallel", "arbitrary"), name="fox_attention",
    )(qt, kv, kv, cum)


def _split3(x):
    hi = x.astype(BF16)
    r = x - hi.astype(F32)
    mid = r.astype(BF16)
    lo = (r - mid.astype(F32)).astype(BF16)
    return hi, mid, lo


def _fox_gate_kernel(f_ref, b_ref, tri_ref, c_ref, carry_ref):
    @pl.when(pl.program_id(1) == 0)
    def _():
        carry_ref[...] = jnp.zeros_like(carry_ref)

    x = f_ref[...] + b_ref[...]
    log_f = jnp.minimum(x, 0.0) - jnp.log(1.0 + jnp.exp(-jnp.abs(x)))
    hi, mid, lo = _split3(log_f)
    tri = tri_ref[...]
    c = _dot(tri, hi) + _dot(tri, mid) + _dot(tri, lo) + carry_ref[...]
    c_ref[...] = c
    carry_ref[...] = c[c.shape[0] - 1:, :]


def _fox_gate_cumsum(f_logit, bias, batch, seqlen, tc=512):
    t, w = f_logit.shape
    nchunk = seqlen // tc
    tri = (lax.broadcasted_iota(jnp.int32, (tc, tc), 1) <= lax.broadcasted_iota(jnp.int32, (tc, tc), 0)).astype(BF16)
    return pl.pallas_call(
        _fox_gate_kernel, grid=(batch, nchunk),
        in_specs=[pl.BlockSpec((tc, w), lambda b, c: (b * nchunk + c, 0)),
                  pl.BlockSpec((1, w), lambda b, c: (0, 0)),
                  pl.BlockSpec((tc, tc), lambda b, c: (0, 0))],
        out_specs=pl.BlockSpec((tc, w), lambda b, c: (b * nchunk + c, 0)),
        out_shape=jax.ShapeDtypeStruct((t, w), F32),
        scratch_shapes=[pltpu.VMEM((1, w), F32)],
        compiler_params=_params("parallel", "arbitrary"), name="fox_gate_cumsum",
    )(f_logit, bias, tri)


def _router_kernel(x_ref, g_ref, whi_ref, wlo_ref, o_ref):
    h = _rms(x_ref[...], g_ref[...])
    h_hi = h.astype(BF16)
    h_lo = (h - h_hi.astype(F32)).astype(BF16)
    logits = _dot(h_hi, whi_ref[...]) + _dot(h_lo, whi_ref[...]) + _dot(h_hi, wlo_ref[...])
    lane = lax.broadcasted_iota(jnp.int32, logits.shape, 1)
    logits = jnp.where(lane < N_EXPERTS, logits, -jnp.inf)
    m1 = jnp.max(logits, axis=1, keepdims=True)
    i1 = jnp.min(jnp.where(logits == m1, lane, LANES), axis=1, keepdims=True)
    rest = jnp.where(lane == i1, -jnp.inf, logits)
    m2 = jnp.max(rest, axis=1, keepdims=True)
    i2 = jnp.min(jnp.where(rest == m2, lane, LANES), axis=1, keepdims=True)
    e2 = jnp.exp(m2 - m1)
    g1 = 1.0 / (1.0 + e2)
    out = jnp.where(lane == 0, g1, 0.0) + jnp.where(lane == 1, e2 * g1, 0.0)
    out = out + jnp.where(lane == 2, i1.astype(F32), 0.0) + jnp.where(lane == 3, i2.astype(F32), 0.0)
    o_ref[...] = out


def _router(x, gain, w_router, tm=1024):
    t, d = x.shape
    w = jnp.zeros((d, LANES), F32).at[:, :N_EXPERTS].set(w_router)
    w_hi = w.astype(BF16)
    w_lo = (w - w_hi.astype(F32)).astype(BF16)
    return pl.pallas_call(
        _router_kernel, grid=(t // tm,),
        in_specs=[pl.BlockSpec((tm, d), lambda i: (i, 0)), pl.BlockSpec((1, d), lambda i: (0, 0)),
                  pl.BlockSpec((d, LANES), lambda i: (0, 0)), pl.BlockSpec((d, LANES), lambda i: (0, 0))],
        out_specs=pl.BlockSpec((tm, LANES), lambda i: (i, 0)),
        out_shape=jax.ShapeDtypeStruct((t, LANES), F32),
        compiler_params=_params("parallel"), name="router",
    )(x, gain.reshape(1, d), w_hi, w_lo)


def _route_plan(route, tm):
    t = route.shape[0]
    n = TOP_K * t
    n_tiles = n // tm
    expert = route[:, 2:2 + TOP_K].astype(jnp.int32).T.reshape(-1)
    onehot = (expert[:, None] == jnp.arange(N_EXPERTS, dtype=jnp.int32)[None, :]).astype(jnp.int32)
    rank = jnp.sum((jnp.cumsum(onehot, axis=0) - onehot) * onehot, axis=1)
    counts = jnp.sum(onehot, axis=0)
    ends = jnp.cumsum(counts)
    pos = (ends - counts)[expert] + rank
    tile_start = jnp.arange(n_tiles, dtype=jnp.int32) * tm
    lo = jnp.sort(jnp.concatenate([tile_start, ends[:-1].astype(jnp.int32)]))
    hi = jnp.concatenate([lo[1:], jnp.full((1,), n, jnp.int32)])
    tile = jnp.minimum(lo // tm, n_tiles - 1)
    exp_id = jnp.minimum(jnp.sum((lo[:, None] >= ends[None, :]).astype(jnp.int32), axis=1), N_EXPERTS - 1)
    first = jnp.concatenate([jnp.ones((1,), jnp.int32), (tile[1:] != tile[:-1]).astype(jnp.int32)])
    last = jnp.concatenate([(tile[1:] != tile[:-1]).astype(jnp.int32), jnp.ones((1,), jnp.int32)])
    visits = jnp.stack([tile, exp_id, lo, hi, first + 2 * last]).astype(jnp.int32)
    return pos.reshape(TOP_K, t).astype(jnp.int32), visits


def _slot_indices(pos, rows):
    t = pos.shape[1]
    return pos.reshape(TOP_K, t // rows, rows).transpose(1, 0, 2).reshape(t // rows, TOP_K * rows)


def _dispatch_kernel(rows, idx_hbm, x_ref, xs_hbm, idx_smem, idx_sem, row_sem):
    idx_copy = pltpu.make_async_copy(idx_hbm.at[pl.program_id(0)], idx_smem, idx_sem)
    idx_copy.start()
    idx_copy.wait()

    def issue(r, _):
        for k in range(TOP_K):
            pltpu.make_async_copy(x_ref.at[pl.ds(r, 1)], xs_hbm.at[pl.ds(idx_smem[k * rows + r], 1)], row_sem).start()
        return 0

    lax.fori_loop(0, rows, issue, 0, unroll=8)
    done = xs_hbm.at[pl.ds(0, TOP_K * rows)]
    pltpu.make_async_copy(done, done, row_sem).wait()


def _dispatch(x, pos, rows=DISPATCH_ROWS):
    t, d = x.shape
    return pl.pallas_call(
        functools.partial(_dispatch_kernel, rows), grid=(t // rows,),
        in_specs=[pl.BlockSpec(memory_space=pl.ANY), pl.BlockSpec((rows, d), lambda i: (i, 0))],
        out_specs=pl.BlockSpec(memory_space=pl.ANY),
        out_shape=jax.ShapeDtypeStruct((TOP_K * t, d), x.dtype),
        scratch_shapes=[pltpu.SMEM((TOP_K * rows,), jnp.int32), pltpu.SemaphoreType.DMA(()),
                        pltpu.SemaphoreType.DMA(())],
        compiler_params=_params("arbitrary"), name="moe_dispatch",
    )(_slot_indices(pos, rows), x)


def _experts_kernel(v_ref, x_ref, g_ref, wg_ref, wu_ref, wd_ref, o_ref, h_ref, acc_ref):
    v, f = pl.program_id(0), pl.program_id(1)
    tm = x_ref.shape[0]
    lo, hi, flags = v_ref[2, v], v_ref[3, v], v_ref[4, v]

    @pl.when(jnp.logical_and((flags & 1) == 1, f == 0))
    def _():
        h_ref[...] = _rms(x_ref[...], g_ref[...]).astype(BF16)
        acc_ref[...] = jnp.zeros_like(acc_ref)

    @pl.when(hi > lo)
    def _():
        h = h_ref[...]
        gate = _dot(h, wg_ref[0])
        up = _dot(h, wu_ref[0])
        row = lax.broadcasted_iota(jnp.int32, (tm, 1), 0) + v_ref[0, v] * tm
        mine = jnp.logical_and(row >= lo, row < hi)
        act = jnp.where(mine, gate * jax.nn.sigmoid(gate) * up, 0.0)
        acc_ref[...] += _dot(act.astype(BF16), wd_ref[0])

    @pl.when(jnp.logical_and((flags & 2) == 2, f == pl.num_programs(1) - 1))
    def _():
        o_ref[...] = acc_ref[...]


def _experts(xs, gain, visits, w_gate, w_up, w_down, tm, tf=512):
    n, d = xs.shape
    ff = w_gate.shape[2]

    def row(v, f, vis):
        return vis[0, v], 0

    grid_spec = pltpu.PrefetchScalarGridSpec(
        num_scalar_prefetch=1, grid=(visits.shape[1], ff // tf),
        in_specs=[pl.BlockSpec((tm, d), row), pl.BlockSpec((1, d), lambda v, f, vis: (0, 0)),
                  pl.BlockSpec((1, d, tf), lambda v, f, vis: (vis[1, v], 0, f)),
                  pl.BlockSpec((1, d, tf), lambda v, f, vis: (vis[1, v], 0, f)),
                  pl.BlockSpec((1, tf, d), lambda v, f, vis: (vis[1, v], f, 0))],
        out_specs=pl.BlockSpec((tm, d), row),
        scratch_shapes=[pltpu.VMEM((tm, d), BF16), pltpu.VMEM((tm, d), F32)])
    return pl.pallas_call(
        _experts_kernel, grid_spec=grid_spec, out_shape=jax.ShapeDtypeStruct((n, d), F32),
        compiler_params=_params("arbitrary", "arbitrary"), name="experts",
    )(visits, xs, gain.reshape(1, d), w_gate, w_up, w_down)


def _combine_kernel(rows, idx_hbm, x_ref, route_ref, g_ref, y_hbm, o_ref, idx_smem, buf, idx_sem, row_sem):
    idx_copy = pltpu.make_async_copy(idx_hbm.at[pl.program_id(0)], idx_smem, idx_sem)
    idx_copy.start()
    idx_copy.wait()

    def issue(r, _):
        for k in range(TOP_K):
            pltpu.make_async_copy(y_hbm.at[pl.ds(idx_smem[k * rows + r], 1)], buf.at[k, pl.ds(r, 1)], row_sem).start()
        return 0

    lax.fori_loop(0, rows, issue, 0, unroll=8)
    pltpu.make_async_copy(buf, buf, row_sem).wait()
    out = x_ref[...]
    for k in range(TOP_K):
        out = out + route_ref[:, k:k + 1] * buf[k]
    o_ref[...] = _rms(out, g_ref[...])


def _combine(x, y, pos, route, gain, rows=COMBINE_ROWS):
    t, d = x.shape
    return pl.pallas_call(
        functools.partial(_combine_kernel, rows), grid=(t // rows,),
        in_specs=[pl.BlockSpec(memory_space=pl.ANY), pl.BlockSpec((rows, d), lambda i: (i, 0)),
                  pl.BlockSpec((rows, LANES), lambda i: (i, 0)), pl.BlockSpec((1, d), lambda i: (0, 0)),
                  pl.BlockSpec(memory_space=pl.ANY)],
        out_specs=pl.BlockSpec((rows, d), lambda i: (i, 0)),
        out_shape=jax.ShapeDtypeStruct((t, d), F32),
        scratch_shapes=[pltpu.SMEM((TOP_K * rows,), jnp.int32), pltpu.VMEM((TOP_K, rows, d), F32),
                        pltpu.SemaphoreType.DMA(()), pltpu.SemaphoreType.DMA(())],
        compiler_params=_params("arbitrary"), name="moe_combine",
    )(_slot_indices(pos, rows), x, route, gain.reshape(1, d), y)


def _swiglu_kernel(x_ref, g_ref, wg_ref, wu_ref, wd_ref, o_ref, h_ref, acc_ref):
    f = pl.program_id(1)

    @pl.when(f == 0)
    def _():
        h_ref[...] = _rms(x_ref[...], g_ref[...]).astype(BF16)
        acc_ref[...] = jnp.zeros_like(acc_ref)

    h = h_ref[...]
    gate = _dot(h, wg_ref[...])
    up = _dot(h, wu_ref[...])
    act = gate * jax.nn.sigmoid(gate) * up
    acc_ref[...] += _dot(act.astype(BF16), wd_ref[...])

    @pl.when(f == pl.num_programs(1) - 1)
    def _():
        o_ref[...] = x_ref[...] + acc_ref[...]


def _swiglu(x, gain, w_gate, w_up, w_down, tm=1024, tf=256):
    t, d = x.shape
    ff = w_gate.shape[1]
    row = pl.BlockSpec((tm, d), lambda i, f: (i, 0))
    return pl.pallas_call(
        _swiglu_kernel, grid=(t // tm, ff // tf),
        in_specs=[row, pl.BlockSpec((1, d), lambda i, f: (0, 0)),
                  pl.BlockSpec((d, tf), lambda i, f: (0, f)), pl.BlockSpec((d, tf), lambda i, f: (0, f)),
                  pl.BlockSpec((tf, d), lambda i, f: (f, 0))],
        out_specs=row, out_shape=jax.ShapeDtypeStruct((t, d), F32),
        scratch_shapes=[pltpu.VMEM((tm, d), BF16), pltpu.VMEM((tm, d), F32)],
        compiler_params=_params("parallel", "arbitrary"), name="swiglu",
    )(x, gain.reshape(1, d), w_gate, w_up, w_down)


def _even_layer(x, batch, seqlen, norm_mix, w_in, s5, w_glu, b_glu, d_skip, w_out, norm_ffn, w_gate, w_up, w_down):
    s5_width = d_skip.size
    sb_width = (w_in.shape[1] - s5_width) // 3
    scale = 1.0 / math.sqrt(SB_HEAD_DIM)
    w_u = w_in[:, :s5_width].astype(BF16)
    w_q = (w_in[:, s5_width:s5_width + sb_width] * scale).astype(BF16)
    w_kv = w_in[:, s5_width + sb_width:].astype(BF16)
    u, qt, kv = _norm_proj(x, norm_mix, [w_u, w_q.T, w_kv], [F32, BF16, BF16], [False, True, False], batch)

    n_scan = int(math.log2(seqlen // S5_CHUNK))
    tables = _s5_tables(*s5, n_scan)
    y = _s5_conv(u, tables, batch, seqlen)
    a_out = _s5_glu(y, u, d_skip, w_glu, b_glu)

    b_out_t = _sb_attention(qt, kv.reshape(batch, seqlen, 2 * sb_width))

    w_out = w_out.astype(BF16)
    x = _proj_residual(x, [a_out, b_out_t], [w_out[:s5_width], w_out[s5_width:]], [False, True], batch)
    return _swiglu(x, norm_ffn, w_gate.astype(BF16), w_up.astype(BF16), w_down.astype(BF16))


def _odd_layer(x, batch, seqlen, norm_mix, w_in, b_f, w_out, norm_ffn, w_router, w_gate, w_up, w_down, final_gain):
    heads = b_f.size
    width = heads * FOX_HEAD_DIM
    scale = LOG2_E / math.sqrt(FOX_HEAD_DIM)
    d = x.shape[1]
    w_q = (w_in[:, :width] * scale).astype(BF16)
    w_kv = w_in[:, width:3 * width].astype(BF16)
    w_f = jnp.zeros((d, LANES), F32).at[:, :heads].set(w_in[:, 3 * width:]).astype(BF16)
    qt, kv, f_logit = _norm_proj(x, norm_mix, [w_q.T, w_kv, w_f], [BF16, BF16, F32], [True, False, False], batch)
    bias = jnp.zeros((1, LANES), F32).at[0, :heads].set(b_f)
    cum = _fox_gate_cumsum(f_logit, bias, batch, seqlen)

    c_out_t = _fox_attention(qt, kv.reshape(batch, seqlen, 2 * width), cum.reshape(batch, seqlen, LANES))
    x = _proj_residual(x, [c_out_t], [w_out.astype(BF16)], [True], batch)
    route = _router(x, norm_ffn, w_router)
    pos, visits = _route_plan(route, MOE_TILE)
    xs = _dispatch(x, pos)
    ys = _experts(xs, norm_ffn, visits, w_gate.astype(BF16), w_up.astype(BF16), w_down.astype(BF16), MOE_TILE)
    return _combine(x, ys, pos, route, final_gain)


def kernel(x, ev_norm_mix, ev_w_in, s5_a_re, s5_a_im, s5_b_re, s5_b_im, s5_c_re, s5_c_im, s5_d, s5_log_step,
           s5_w_glu, s5_b_glu, ev_w_out, ev_norm_ffn, ffn_w_gate, ffn_w_up, ffn_w_down, od_norm_mix, od_w_in,
           fox_b_f, od_w_out, od_norm_ffn, moe_w_router, moe_w_gate, moe_w_up, moe_w_down, final_norm):
    batch, seqlen, d = x.shape
    assert ev_w_in.shape[0] == 1 and od_w_in.shape[0] == 1, "one even and one odd layer"
    h = x.reshape(batch * seqlen, d)
    s5 = (s5_a_re[0], s5_a_im[0], s5_b_re[0], s5_b_im[0], s5_c_re[0], s5_c_im[0], s5_log_step[0])
    h = _even_layer(h, batch, seqlen, ev_norm_mix[0], ev_w_in[0], s5, s5_w_glu[0], s5_b_glu[0], s5_d[0].reshape(-1),
                    ev_w_out[0], ev_norm_ffn[0], ffn_w_gate[0], ffn_w_up[0], ffn_w_down[0])
    h = _odd_layer(h, batch, seqlen, od_norm_mix[0], od_w_in[0], fox_b_f[0], od_w_out[0], od_norm_ffn[0],
                   moe_w_router[0], moe_w_gate[0], moe_w_up[0], moe_w_down[0], final_norm)
    return h.reshape(batch, seqlen, d)
```

```python
import functools
import math

import jax
import jax.numpy as jnp
from jax import lax
from jax.experimental import pallas as pl
from jax.experimental.pallas import tpu as pltpu

F32 = jnp.float32
BF16 = jnp.bfloat16

RMS_EPS = 1e-6
LANES = 128
VMEM_LIMIT_BYTES = 56 * 1024 * 1024

S5_GROUP = 16
S5_STATE = 64
S5_CHUNK = 16
SB_HEAD_DIM = 64
FOX_HEAD_DIM = 64
N_EXPERTS = 8
TOP_K = 2
MOE_TILE = 1024
DISPATCH_ROWS = 512
COMBINE_ROWS = 512
MXU_TILE = 256
ATT_BQ = 1024
ATT_BK = 512
LOG2_E = math.log2(math.e)


def _params(*sem):
    return pltpu.CompilerParams(dimension_semantics=sem, vmem_limit_bytes=VMEM_LIMIT_BYTES)


def _rms(x, g):
    ms = jnp.mean(x * x, axis=-1, keepdims=True)
    return x * lax.rsqrt(ms + RMS_EPS) * g


def _dot(a, b):
    return jnp.dot(a, b, preferred_element_type=F32)


_CONTRACT_LAST = (((1,), (1,)), ((), ()))
_CONTRACT_FIRST = (((0,), (0,)), ((), ()))


def _norm_proj_kernel(feature_major, x_ref, g_ref, *refs):
    n_out = len(feature_major)
    w_refs, o_refs = refs[:n_out], refs[n_out:]
    h = _rms(x_ref[...], g_ref[...]).astype(BF16)
    for fm, w_ref, o_ref in zip(feature_major, w_refs, o_refs):
        if fm:
            o_ref[0] = lax.dot_general(w_ref[...], h, _CONTRACT_LAST, preferred_element_type=F32).astype(o_ref.dtype)
        else:
            o_ref[...] = _dot(h, w_ref[...]).astype(o_ref.dtype)


def _norm_proj(x, gain, weights, out_dtypes, feature_major, batch, tm=512):
    t, d = x.shape
    per_seq = t // batch // tm
    in_specs = [pl.BlockSpec((tm, d), lambda i: (i, 0)), pl.BlockSpec((1, d), lambda i: (0, 0))]
    in_specs += [pl.BlockSpec(w.shape, lambda i: (0, 0)) for w in weights]
    out_specs, out_shape = [], []
    for w, dt, fm in zip(weights, out_dtypes, feature_major):
        if fm:
            out_specs.append(pl.BlockSpec((1, w.shape[0], tm), lambda i: (i // per_seq, 0, i % per_seq)))
            out_shape.append(jax.ShapeDtypeStruct((batch, w.shape[0], t // batch), dt))
        else:
            out_specs.append(pl.BlockSpec((tm, w.shape[1]), lambda i: (i, 0)))
            out_shape.append(jax.ShapeDtypeStruct((t, w.shape[1]), dt))
    return pl.pallas_call(
        functools.partial(_norm_proj_kernel, tuple(feature_major)),
        grid=(t // tm,), in_specs=in_specs, out_specs=out_specs, out_shape=out_shape,
        compiler_params=_params("parallel"), name="norm_proj",
    )(x, gain.reshape(1, d), *weights)


def _proj_residual_kernel(feature_major, res_ref, *refs):
    n_in = len(feature_major)
    a_refs, w_refs, o_ref = refs[:n_in], refs[n_in:2 * n_in], refs[2 * n_in]
    acc = res_ref[...]
    for fm, a_ref, w_ref in zip(feature_major, a_refs, w_refs):
        if fm:
            acc = acc + lax.dot_general(a_ref[0], w_ref[...], _CONTRACT_FIRST, preferred_element_type=F32)
        else:
            acc = acc + _dot(a_ref[...], w_ref[...])
    o_ref[...] = acc


def _proj_residual(res, acts, weights, feature_major, batch, tm=512):
    t, n = res.shape
    per_seq = t // batch // tm
    in_specs = [pl.BlockSpec((tm, n), lambda i: (i, 0))]
    for a, fm in zip(acts, feature_major):
        if fm:
            in_specs.append(pl.BlockSpec((1, a.shape[1], tm), lambda i: (i // per_seq, 0, i % per_seq)))
        else:
            in_specs.append(pl.BlockSpec((tm, a.shape[1]), lambda i: (i, 0)))
    in_specs += [pl.BlockSpec(w.shape, lambda i: (0, 0)) for w in weights]
    return pl.pallas_call(
        functools.partial(_proj_residual_kernel, tuple(feature_major)),
        grid=(t // tm,), in_specs=in_specs, out_specs=pl.BlockSpec((tm, n), lambda i: (i, 0)),
        out_shape=jax.ShapeDtypeStruct((t, n), F32),
        compiler_params=_params("parallel"), name="proj_residual",
    )(res, *acts, *weights)


def _s5_tables(a_re, a_im, b_re, b_im, c_re, c_im, log_step, n_scan):
    g, p = a_re.shape
    h = b_re.shape[-1]
    ch = S5_CHUNK
    lam = lax.complex(a_re, a_im)
    dl = lam * jnp.exp(log_step)[:, None]
    lam_bar = jnp.exp(dl)
    b_bar = ((lam_bar - 1.0) / lam)[..., None] * lax.complex(b_re, b_im)
    c = lax.complex(c_re, c_im)
    steps = jnp.arange(ch + 1, dtype=a_re.dtype)
    pw = jnp.exp(dl[:, None, :] * steps[None, :, None])
    kd = jnp.einsum('gop,gdp,gpi->gdoi', c, pw[:, :ch], b_bar).real
    lag = jnp.arange(ch)[None, :] - jnp.arange(ch)[:, None]
    kfull = kd[:, jnp.clip(lag, 0, ch - 1)]
    kfull = jnp.where((lag >= 0)[None, :, :, None, None], kfull, 0.0)
    k_tab = kfull.transpose(0, 1, 4, 2, 3).reshape(g, ch * h, ch * h)
    bc = pw[:, ch - 1 - jnp.arange(ch)][:, :, :, None] * b_bar[:, None]
    bc = bc.transpose(0, 1, 3, 2).reshape(g, ch * h, p)
    b_tab = jnp.concatenate([bc.real, bc.imag], axis=-1)
    m = c[:, None] * pw[:, 1:ch + 1][:, :, None, :]
    m = m.transpose(0, 3, 1, 2).reshape(g, p, ch * h)
    c_tab = jnp.concatenate([m.real, -m.imag], axis=1)
    pows = (ch * (2 ** jnp.arange(n_scan))).astype(a_re.dtype)
    ak = jnp.exp(dl[:, None, :] * pows[None, :, None])
    p1 = jnp.concatenate([ak.real, ak.real], axis=-1)
    p2 = jnp.concatenate([-ak.imag, ak.imag], axis=-1)
    return k_tab, b_tab, c_tab, p1, p2


def _s5_kernel(n_scan, rows_per_seq, u_ref, k_ref, b_ref, c_ref, p1_ref, p2_ref, y_ref):
    u = u_ref[0]
    state = _dot(u, b_ref[0])
    row = lax.broadcasted_iota(jnp.int32, state.shape, 0) & (rows_per_seq - 1)
    half = state.shape[1] // 2
    for k in range(n_scan):
        shift = 1 << k
        prev = jnp.where(row >= shift, pltpu.roll(state, shift, axis=0), 0.0)
        state = state + prev * p1_ref[0, k:k + 1, :] + pltpu.roll(prev, half, axis=1) * p2_ref[0, k:k + 1, :]
    carried = jnp.where(row >= 1, pltpu.roll(state, 1, axis=0), 0.0)
    y_ref[0] = _dot(u, k_ref[0]) + _dot(carried.astype(BF16), c_ref[0])


def _s5_conv(u, tables, batch, seqlen):
    k_tab, b_tab, c_tab, p1, p2 = tables
    t, width = u.shape
    g = width // S5_GROUP
    ch = S5_CHUNK
    rows = t // ch
    rows_per_seq = seqlen // ch
    n_scan = p1.shape[1]
    ug = u.reshape(rows, ch, g, S5_GROUP).transpose(2, 0, 1, 3).reshape(g, rows, ch * S5_GROUP).astype(BF16)
    kw = ch * S5_GROUP
    sw = b_tab.shape[-1]
    y = pl.pallas_call(
        functools.partial(_s5_kernel, n_scan, rows_per_seq),
        grid=(g,),
        in_specs=[pl.BlockSpec((1, rows, kw), lambda i: (i, 0, 0)),
                  pl.BlockSpec((1, kw, kw), lambda i: (i, 0, 0)),
                  pl.BlockSpec((1, kw, sw), lambda i: (i, 0, 0)),
                  pl.BlockSpec((1, sw, kw), lambda i: (i, 0, 0)),
                  pl.BlockSpec((1, n_scan, sw), lambda i: (i, 0, 0)),
                  pl.BlockSpec((1, n_scan, sw), lambda i: (i, 0, 0))],
        out_specs=pl.BlockSpec((1, rows, kw), lambda i: (i, 0, 0)),
        out_shape=jax.ShapeDtypeStruct((g, rows, kw), F32),
        compiler_params=_params("parallel"), name="s5_conv",
    )(ug, k_tab.astype(BF16), b_tab.astype(BF16), c_tab.astype(BF16), p1, p2)
    return y.reshape(g, rows, ch, S5_GROUP).transpose(1, 2, 0, 3).reshape(t, width)


def _gelu_tanh(x):
    return 0.5 * x * (1.0 + jnp.tanh(math.sqrt(2.0 / math.pi) * (x + 0.044715 * (x * x * x))))


def _s5_glu_kernel(y_ref, u_ref, d_ref, w_ref, b_ref, o_ref):
    y = _gelu_tanh(y_ref[...] + d_ref[...] * u_ref[...])
    gate = jax.nn.sigmoid(_dot(y.astype(BF16), w_ref[...]) + b_ref[...])
    o_ref[...] = (y * gate).astype(o_ref.dtype)


def _s5_glu(y, u, d_skip, w_glu, b_glu, tm=1024):
    t, width = y.shape
    row = pl.BlockSpec((tm, width), lambda i: (i, 0))
    vec = pl.BlockSpec((1, width), lambda i: (0, 0))
    return pl.pallas_call(
        _s5_glu_kernel, grid=(t // tm,),
        in_specs=[row, row, vec, pl.BlockSpec((width, width), lambda i: (0, 0)), vec],
        out_specs=row, out_shape=jax.ShapeDtypeStruct((t, width), BF16),
        compiler_params=_params("parallel"), name="s5_glu",
    )(y, u, d_skip.reshape(1, width), w_glu.astype(BF16), b_glu.reshape(1, width))


def _diag_valid(bk, bq, d, strict):
    key = lax.broadcasted_iota(jnp.int32, (bk, bq), 0) + d * bk
    qry = lax.broadcasted_iota(jnp.int32, (bk, bq), 1)
    return key < qry if strict else key <= qry


def _softplus2(z):
    return jnp.maximum(z, 0.0) + jnp.log2(1.0 + jnp.exp2(-jnp.abs(z)))


def _key_rows(j, bk):
    return pl.ds(pl.multiple_of(j * bk, bk), bk)


def _sb_kernel(bk, qt_ref, k_ref, v_ref, tri_ref, ot_ref):
    qi = pl.program_id(2)
    width, bq = qt_ref.shape[1], qt_ref.shape[2]
    dh = width // 2
    tb = tri_ref.shape[0]
    ratio = bq // bk
    feat = lax.broadcasted_iota(jnp.int32, (width, bq), 0)
    qt2 = qt_ref[0]

    def head(p):
        qt = jnp.where((feat >= p * dh) & (feat < (p + 1) * dh), qt2, jnp.zeros_like(qt2))

        def block(j, carry, acc, diag):
            rows = _key_rows(j, bk)
            z = _dot(k_ref[0, rows, :], qt)
            sp = _softplus2(z)
            if diag is not None:
                valid = _diag_valid(bk, bq, diag, True)
                sp = jnp.where(valid, sp, 0.0)
            sp = sp.astype(BF16)
            parts = []
            for t in reversed(range(bk // tb)):
                sub = slice(t * tb, (t + 1) * tb)
                part = _dot(tri_ref[...], sp[sub]) + carry
                carry = part[0:1, :]
                parts.insert(0, part)
            w = jnp.exp2(z - jnp.concatenate(parts, axis=0))
            if diag is not None:
                w = jnp.where(valid, w, 0.0)
            pv = lax.dot_general(v_ref[0, rows, :], w.astype(BF16), _CONTRACT_FIRST, preferred_element_type=F32)
            return carry, acc + pv

        carry = jnp.zeros((1, bq), F32)
        acc = jnp.zeros((width, bq), F32)
        for d in reversed(range(ratio)):
            carry, acc = block(qi * ratio + d, carry, acc, d)
        carry, acc = lax.fori_loop(0, qi * ratio, lambda i, c: block(qi * ratio - 1 - i, c[0], c[1], None),
                                   (carry, acc))
        return acc

    acc0, acc1 = head(0), head(1)
    ot_ref[0] = jnp.where(feat < dh, acc0, acc1).astype(ot_ref.dtype)


def _sb_attention(qt, kv, bq=ATT_BQ, bk=ATT_BK):
    batch, width, seqlen = qt.shape
    pairs = width // LANES
    tb = MXU_TILE
    tri = (lax.broadcasted_iota(jnp.int32, (tb, tb), 0) <= lax.broadcasted_iota(jnp.int32, (tb, tb), 1)).astype(BF16)
    qspec = pl.BlockSpec((1, LANES, bq), lambda b, h, q: (b, h, q))
    return pl.pallas_call(
        functools.partial(_sb_kernel, bk), grid=(batch, pairs, seqlen // bq),
        in_specs=[qspec,
                  pl.BlockSpec((1, seqlen, LANES), lambda b, h, q: (b, 0, h)),
                  pl.BlockSpec((1, seqlen, LANES), lambda b, h, q: (b, 0, pairs + h)),
                  pl.BlockSpec((tb, tb), lambda b, h, q: (0, 0))],
        out_specs=qspec, out_shape=jax.ShapeDtypeStruct(qt.shape, BF16),
        compiler_params=_params("parallel", "parallel", "arbitrary"), name="sb_attention",
    )(qt, kv, kv, tri)


def _fox_kernel(bk, qt_ref, k_ref, v_ref, cum_ref, ot_ref, bias_ref):
    hp, qi = pl.program_id(1), pl.program_id(2)
    width, bq = qt_ref.shape[1], qt_ref.shape[2]
    dh = width // 2
    ratio = bq // bk
    feat = lax.broadcasted_iota(jnp.int32, (width, bq), 0)
    vlane = lax.broadcasted_iota(jnp.int32, (bk, width), 1)
    qt2 = qt_ref[0]

    @pl.when(qi == 0)
    def _():
        c = cum_ref[0]
        head_lane = lax.broadcasted_iota(jnp.int32, c.shape, 1)
        for p in range(2):
            col = jnp.sum(jnp.where(head_lane == 2 * hp + p, c, 0.0), axis=1, keepdims=True)
            bias_ref[p] = jnp.broadcast_to(-LOG2_E * col, c.shape)

    def head(p):
        mine = (feat >= p * dh) & (feat < (p + 1) * dh)
        qt = jnp.where(mine, qt2, jnp.zeros_like(qt2))
        vmine = (vlane >= p * dh) & (vlane < (p + 1) * dh)

        def block(j, m, acc, diag):
            rows = _key_rows(j, bk)
            bias = bias_ref[p, rows, :]
            s = _dot(k_ref[0, rows, :], qt) + jnp.concatenate([bias] * (bq // LANES), axis=1)
            if diag is not None:
                s = jnp.where(_diag_valid(bk, bq, diag, False), s, -jnp.inf)
            m_new = jnp.maximum(m, jnp.max(s, axis=0, keepdims=True))
            prob = jnp.exp2(s - m_new).astype(BF16)
            v = v_ref[0, rows, :]
            v = jnp.where(vmine, v, jnp.ones_like(v))
            pv = lax.dot_general(v, prob, _CONTRACT_FIRST, preferred_element_type=F32)
            return m_new, jnp.exp2(m - m_new) * acc + pv

        m = jnp.full((1, bq), -jnp.inf, F32)
        acc = jnp.zeros((width, bq), F32)
        m, acc = lax.fori_loop(0, qi * ratio, lambda j, c: block(j, c[0], c[1], None), (m, acc))
        for d in range(ratio):
            m, acc = block(qi * ratio + d, m, acc, d)
        other = (1 - p) * dh
        return acc / acc[other:other + 1]

    out0, out1 = head(0), head(1)
    ot_ref[0] = jnp.where(feat < dh, out0, out1).astype(ot_ref.dtype)


def _fox_attention(qt, kv, cum, bq=ATT_BQ, bk=ATT_BK):
    batch, width, seqlen = qt.shape
    pairs = width // LANES
    qspec = pl.BlockSpec((1, LANES, bq), lambda b, h, q: (b, h, q))
    return pl.pallas_call(
        functools.partial(_fox_kernel, bk), grid=(batch, pairs, seqlen // bq),
        in_specs=[qspec,
                  pl.BlockSpec((1, seqlen, LANES), lambda b, h, q: (b, 0, h)),
                  pl.BlockSpec((1, seqlen, LANES), lambda b, h, q: (b, 0, pairs + h)),
                  pl.BlockSpec((1, seqlen, LANES), lambda b, h, q: (b, 0, 0))],
        out_specs=qspec, out_shape=jax.ShapeDtypeStruct(qt.shape, BF16),
        scratch_shapes=[pltpu.VMEM((2, seqlen, LANES), F32)],
        compiler_params=_params("parallel", "parallel", "arbitrary"), name="fox_attention",
    )(qt, kv, kv, cum)


def _split3(x):
    hi = x.astype(BF16)
    r = x - hi.astype(F32)
    mid = r.astype(BF16)
    lo = (r - mid.astype(F32)).astype(BF16)
    return hi, mid, lo


def _fox_gate_kernel(f_ref, b_ref, tri_ref, c_ref, carry_ref):
    @pl.when(pl.program_id(1) == 0)
    def _():
        carry_ref[...] = jnp.zeros_like(carry_ref)

    x = f_ref[...] + b_ref[...]
    log_f = jnp.minimum(x, 0.0) - jnp.log(1.0 + jnp.exp(-jnp.abs(x)))
    hi, mid, lo = _split3(log_f)
    tri = tri_ref[...]
    c = _dot(tri, hi) + _dot(tri, mid) + _dot(tri, lo) + carry_ref[...]
    c_ref[...] = c
    carry_ref[...] = c[c.shape[0] - 1:, :]


def _fox_gate_cumsum(f_logit, bias, batch, seqlen, tc=512):
    t, w = f_logit.shape
    nchunk = seqlen // tc
    tri = (lax.broadcasted_iota(jnp.int32, (tc, tc), 1) <= lax.broadcasted_iota(jnp.int32, (tc, tc), 0)).astype(BF16)
    return pl.pallas_call(
        _fox_gate_kernel, grid=(batch, nchunk),
        in_specs=[pl.BlockSpec((tc, w), lambda b, c: (b * nchunk + c, 0)),
                  pl.BlockSpec((1, w), lambda b, c: (0, 0)),
                  pl.BlockSpec((tc, tc), lambda b, c: (0, 0))],
        out_specs=pl.BlockSpec((tc, w), lambda b, c: (b * nchunk + c, 0)),
        out_shape=jax.ShapeDtypeStruct((t, w), F32),
        scratch_shapes=[pltpu.VMEM((1, w), F32)],
        compiler_params=_params("parallel", "arbitrary"), name="fox_gate_cumsum",
    )(f_logit, bias, tri)


def _router_kernel(x_ref, g_ref, whi_ref, wlo_ref, o_ref):
    h = _rms(x_ref[...], g_ref[...])
    h_hi = h.astype(BF16)
    h_lo = (h - h_hi.astype(F32)).astype(BF16)
    logits = _dot(h_hi, whi_ref[...]) + _dot(h_lo, whi_ref[...]) + _dot(h_hi, wlo_ref[...])
    lane = lax.broadcasted_iota(jnp.int32, logits.shape, 1)
    logits = jnp.where(lane < N_EXPERTS, logits, -jnp.inf)
    m1 = jnp.max(logits, axis=1, keepdims=True)
    i1 = jnp.min(jnp.where(logits == m1, lane, LANES), axis=1, keepdims=True)
    rest = jnp.where(lane == i1, -jnp.inf, logits)
    m2 = jnp.max(rest, axis=1, keepdims=True)
    i2 = jnp.min(jnp.where(rest == m2, lane, LANES), axis=1, keepdims=True)
    e2 = jnp.exp(m2 - m1)
    g1 = 1.0 / (1.0 + e2)
    out = jnp.where(lane == 0, g1, 0.0) + jnp.where(lane == 1, e2 * g1, 0.0)
    out = out + jnp.where(lane == 2, i1.astype(F32), 0.0) + jnp.where(lane == 3, i2.astype(F32), 0.0)
    o_ref[...] = out


def _router(x, gain, w_router, tm=1024):
    t, d = x.shape
    w = jnp.zeros((d, LANES), F32).at[:, :N_EXPERTS].set(w_router)
    w_hi = w.astype(BF16)
    w_lo = (w - w_hi.astype(F32)).astype(BF16)
    return pl.pallas_call(
        _router_kernel, grid=(t // tm,),
        in_specs=[pl.BlockSpec((tm, d), lambda i: (i, 0)), pl.BlockSpec((1, d), lambda i: (0, 0)),
                  pl.BlockSpec((d, LANES), lambda i: (0, 0)), pl.BlockSpec((d, LANES), lambda i: (0, 0))],
        out_specs=pl.BlockSpec((tm, LANES), lambda i: (i, 0)),
        out_shape=jax.ShapeDtypeStruct((t, LANES), F32),
        compiler_params=_params("parallel"), name="router",
    )(x, gain.reshape(1, d), w_hi, w_lo)


def _route_plan(route, tm):
    t = route.shape[0]
    n = TOP_K * t
    n_tiles = n // tm
    expert = route[:, 2:2 + TOP_K].astype(jnp.int32).T.reshape(-1)
    onehot = (expert[:, None] == jnp.arange(N_EXPERTS, dtype=jnp.int32)[None, :]).astype(jnp.int32)
    rank = jnp.sum((jnp.cumsum(onehot, axis=0) - onehot) * onehot, axis=1)
    counts = jnp.sum(onehot, axis=0)
    ends = jnp.cumsum(counts)
    pos = (ends - counts)[expert] + rank
    tile_start = jnp.arange(n_tiles, dtype=jnp.int32) * tm
    lo = jnp.sort(jnp.concatenate([tile_start, ends[:-1].astype(jnp.int32)]))
    hi = jnp.concatenate([lo[1:], jnp.full((1,), n, jnp.int32)])
    tile = jnp.minimum(lo // tm, n_tiles - 1)
    exp_id = jnp.minimum(jnp.sum((lo[:, None] >= ends[None, :]).astype(jnp.int32), axis=1), N_EXPERTS - 1)
    first = jnp.concatenate([jnp.ones((1,), jnp.int32), (tile[1:] != tile[:-1]).astype(jnp.int32)])
    last = jnp.concatenate([(tile[1:] != tile[:-1]).astype(jnp.int32), jnp.ones((1,), jnp.int32)])
    visits = jnp.stack([tile, exp_id, lo, hi, first + 2 * last]).astype(jnp.int32)
    return pos.reshape(TOP_K, t).astype(jnp.int32), visits


def _slot_indices(pos, rows):
    t = pos.shape[1]
    return pos.reshape(TOP_K, t // rows, rows).transpose(1, 0, 2).reshape(t // rows, TOP_K * rows)


def _dispatch_kernel(rows, idx_hbm, x_ref, xs_hbm, idx_smem, idx_sem, row_sem):
    idx_copy = pltpu.make_async_copy(idx_hbm.at[pl.program_id(0)], idx_smem, idx_sem)
    idx_copy.start()
    idx_copy.wait()

    def issue(r, _):
        for k in range(TOP_K):
            pltpu.make_async_copy(x_ref.at[pl.ds(r, 1)], xs_hbm.at[pl.ds(idx_smem[k * rows + r], 1)], row_sem).start()
        return 0

    lax.fori_loop(0, rows, issue, 0, unroll=8)
    done = xs_hbm.at[pl.ds(0, TOP_K * rows)]
    pltpu.make_async_copy(done, done, row_sem).wait()


def _dispatch(x, pos, rows=DISPATCH_ROWS):
    t, d = x.shape
    return pl.pallas_call(
        functools.partial(_dispatch_kernel, rows), grid=(t // rows,),
        in_specs=[pl.BlockSpec(memory_space=pl.ANY), pl.BlockSpec((rows, d), lambda i: (i, 0))],
        out_specs=pl.BlockSpec(memory_space=pl.ANY),
        out_shape=jax.ShapeDtypeStruct((TOP_K * t, d), x.dtype),
        scratch_shapes=[pltpu.SMEM((TOP_K * rows,), jnp.int32), pltpu.SemaphoreType.DMA(()),
                        pltpu.SemaphoreType.DMA(())],
        compiler_params=_params("arbitrary"), name="moe_dispatch",
    )(_slot_indices(pos, rows), x)


def _experts_kernel(v_ref, x_ref, g_ref, wg_ref, wu_ref, wd_ref, o_ref, h_ref, acc_ref):
    v, f = pl.program_id(0), pl.program_id(1)
    tm = x_ref.shape[0]
    lo, hi, flags = v_ref[2, v], v_ref[3, v], v_ref[4, v]

    @pl.when(jnp.logical_and((flags & 1) == 1, f == 0))
    def _():
        h_ref[...] = _rms(x_ref[...], g_ref[...]).astype(BF16)
        acc_ref[...] = jnp.zeros_like(acc_ref)

    @pl.when(hi > lo)
    def _():
        h = h_ref[...]
        gate = _dot(h, wg_ref[0])
        up = _dot(h, wu_ref[0])
        row = lax.broadcasted_iota(jnp.int32, (tm, 1), 0) + v_ref[0, v] * tm
        mine = jnp.logical_and(row >= lo, row < hi)
        act = jnp.where(mine, gate * jax.nn.sigmoid(gate) * up, 0.0)
        acc_ref[...] += _dot(act.astype(BF16), wd_ref[0])

    @pl.when(jnp.logical_and((flags & 2) == 2, f == pl.num_programs(1) - 1))
    def _():
        o_ref[...] = acc_ref[...]


def _experts(xs, gain, visits, w_gate, w_up, w_down, tm, tf=512):
    n, d = xs.shape
    ff = w_gate.shape[2]

    def row(v, f, vis):
        return vis[0, v], 0

    grid_spec = pltpu.PrefetchScalarGridSpec(
        num_scalar_prefetch=1, grid=(visits.shape[1], ff // tf),
        in_specs=[pl.BlockSpec((tm, d), row), pl.BlockSpec((1, d), lambda v, f, vis: (0, 0)),
                  pl.BlockSpec((1, d, tf), lambda v, f, vis: (vis[1, v], 0, f)),
                  pl.BlockSpec((1, d, tf), lambda v, f, vis: (vis[1, v], 0, f)),
                  pl.BlockSpec((1, tf, d), lambda v, f, vis: (vis[1, v], f, 0))],
        out_specs=pl.BlockSpec((tm, d), row),
        scratch_shapes=[pltpu.VMEM((tm, d), BF16), pltpu.VMEM((tm, d), F32)])
    return pl.pallas_call(
        _experts_kernel, grid_spec=grid_spec, out_shape=jax.ShapeDtypeStruct((n, d), F32),
        compiler_params=_params("arbitrary", "arbitrary"), name="experts",
    )(visits, xs, gain.reshape(1, d), w_gate, w_up, w_down)


def _combine_kernel(rows, idx_hbm, x_ref, route_ref, g_ref, y_hbm, o_ref, idx_smem, buf, idx_sem, row_sem):
    idx_copy = pltpu.make_async_copy(idx_hbm.at[pl.program_id(0)], idx_smem, idx_sem)
    idx_copy.start()
    idx_copy.wait()

    def issue(r, _):
        for k in range(TOP_K):
            pltpu.make_async_copy(y_hbm.at[pl.ds(idx_smem[k * rows + r], 1)], buf.at[k, pl.ds(r, 1)], row_sem).start()
        return 0

    lax.fori_loop(0, rows, issue, 0, unroll=8)
    pltpu.make_async_copy(buf, buf, row_sem).wait()
    out = x_ref[...]
    for k in range(TOP_K):
        out = out + route_ref[:, k:k + 1] * buf[k]
    o_ref[...] = _rms(out, g_ref[...])


def _combine(x, y, pos, route, gain, rows=COMBINE_ROWS):
    t, d = x.shape
    return pl.pallas_call(
        functools.partial(_combine_kernel, rows), grid=(t // rows,),
        in_specs=[pl.BlockSpec(memory_space=pl.ANY), pl.BlockSpec((rows, d), lambda i: (i, 0)),
                  pl.BlockSpec((rows, LANES), lambda i: (i, 0)), pl.BlockSpec((1, d), lambda i: (0, 0)),
                  pl.BlockSpec(memory_space=pl.ANY)],
        out_specs=pl.BlockSpec((rows, d), lambda i: (i, 0)),
        out_shape=jax.ShapeDtypeStruct((t, d), F32),
        scratch_shapes=[pltpu.SMEM((TOP_K * rows,), jnp.int32), pltpu.VMEM((TOP_K, rows, d), F32),
                        pltpu.SemaphoreType.DMA(()), pltpu.SemaphoreType.DMA(())],
        compiler_params=_params("arbitrary"), name="moe_combine",
    )(_slot_indices(pos, rows), x, route, gain.reshape(1, d), y)


def _swiglu_kernel(x_ref, g_ref, wg_ref, wu_ref, wd_ref, o_ref, h_ref, acc_ref):
    f = pl.program_id(1)

    @pl.when(f == 0)
    def _():
        h_ref[...] = _rms(x_ref[...], g_ref[...]).astype(BF16)
        acc_ref[...] = jnp.zeros_like(acc_ref)

    h = h_ref[...]
    gate = _dot(h, wg_ref[...])
    up = _dot(h, wu_ref[...])
    act = gate * jax.nn.sigmoid(gate) * up
    acc_ref[...] += _dot(act.astype(BF16), wd_ref[...])

    @pl.when(f == pl.num_programs(1) - 1)
    def _():
        o_ref[...] = x_ref[...] + acc_ref[...]


def _swiglu(x, gain, w_gate, w_up, w_down, tm=1024, tf=256):
    t, d = x.shape
    ff = w_gate.shape[1]
    row = pl.BlockSpec((tm, d), lambda i, f: (i, 0))
    return pl.pallas_call(
        _swiglu_kernel, grid=(t // tm, ff // tf),
        in_specs=[row, pl.BlockSpec((1, d), lambda i, f: (0, 0)),
                  pl.BlockSpec((d, tf), lambda i, f: (0, f)), pl.BlockSpec((d, tf), lambda i, f: (0, f)),
                  pl.BlockSpec((tf, d), lambda i, f: (f, 0))],
        out_specs=row, out_shape=jax.ShapeDtypeStruct((t, d), F32),
        scratch_shapes=[pltpu.VMEM((tm, d), BF16), pltpu.VMEM((tm, d), F32)],
        compiler_params=_params("parallel", "arbitrary"), name="swiglu",
    )(x, gain.reshape(1, d), w_gate, w_up, w_down)


def _even_layer(x, batch, seqlen, norm_mix, w_in, s5, w_glu, b_glu, d_skip, w_out, norm_ffn, w_gate, w_up, w_down):
    s5_width = d_skip.size
    sb_width = (w_in.shape[1] - s5_width) // 3
    scale = LOG2_E / math.sqrt(SB_HEAD_DIM)
    w_u = w_in[:, :s5_width].astype(BF16)
    w_q = (w_in[:, s5_width:s5_width + sb_width] * scale).astype(BF16)
    w_kv = w_in[:, s5_width + sb_width:].astype(BF16)
    u, qt, kv = _norm_proj(x, norm_mix, [w_u, w_q.T, w_kv], [F32, BF16, BF16], [False, True, False], batch)

    n_scan = int(math.log2(seqlen // S5_CHUNK))
    tables = _s5_tables(*s5, n_scan)
    y = _s5_conv(u, tables, batch, seqlen)
    a_out = _s5_glu(y, u, d_skip, w_glu, b_glu)

    b_out_t = _sb_attention(qt, kv.reshape(batch, seqlen, 2 * sb_width))

    w_out = w_out.astype(BF16)
    x = _proj_residual(x, [a_out, b_out_t], [w_out[:s5_width], w_out[s5_width:]], [False, True], batch)
    return _swiglu(x, norm_ffn, w_gate.astype(BF16), w_up.astype(BF16), w_down.astype(BF16))


def _odd_layer(x, batch, seqlen, norm_mix, w_in, b_f, w_out, norm_ffn, w_router, w_gate, w_up, w_down, final_gain):
    heads = b_f.size
    width = heads * FOX_HEAD_DIM
    scale = LOG2_E / math.sqrt(FOX_HEAD_DIM)
    d = x.shape[1]
    w_q = (w_in[:, :width] * scale).astype(BF16)
    w_kv = w_in[:, width:3 * width].astype(BF16)
    w_f = jnp.zeros((d, LANES), F32).at[:, :heads].set(w_in[:, 3 * width:]).astype(BF16)
    qt, kv, f_logit = _norm_proj(x, norm_mix, [w_q.T, w_kv, w_f], [BF16, BF16, F32], [True, False, False], batch)
    bias = jnp.zeros((1, LANES), F32).at[0, :heads].set(b_f)
    cum = _fox_gate_cumsum(f_logit, bias, batch, seqlen)

    c_out_t = _fox_attention(qt, kv.reshape(batch, seqlen, 2 * width), cum.reshape(batch, seqlen, LANES))
    x = _proj_residual(x, [c_out_t], [w_out.astype(BF16)], [True], batch)
    route = _router(x, norm_ffn, w_router)
    pos, visits = _route_plan(route, MOE_TILE)
    xs = _dispatch(x, pos)
    ys = _experts(xs, norm_ffn, visits, w_gate.astype(BF16), w_up.astype(BF16), w_down.astype(BF16), MOE_TILE)
    return _combine(x, ys, pos, route, final_gain)


def kernel(x, ev_norm_mix, ev_w_in, s5_a_re, s5_a_im, s5_b_re, s5_b_im, s5_c_re, s5_c_im, s5_d, s5_log_step,
           s5_w_glu, s5_b_glu, ev_w_out, ev_norm_ffn, ffn_w_gate, ffn_w_up, ffn_w_down, od_norm_mix, od_w_in,
           fox_b_f, od_w_out, od_norm_ffn, moe_w_router, moe_w_gate, moe_w_up, moe_w_down, final_norm):
    batch, seqlen, d = x.shape
    assert ev_w_in.shape[0] == 1 and od_w_in.shape[0] == 1, "one even and one odd layer"
    h = x.reshape(batch * seqlen, d)
    s5 = (s5_a_re[0], s5_a_im[0], s5_b_re[0], s5_b_im[0], s5_c_re[0], s5_c_im[0], s5_log_step[0])
    h = _even_layer(h, batch, seqlen, ev_norm_mix[0], ev_w_in[0], s5, s5_w_glu[0], s5_b_glu[0], s5_d[0].reshape(-1),
                    ev_w_out[0], ev_norm_ffn[0], ffn_w_gate[0], ffn_w_up[0], ffn_w_down[0])
    h = _odd_layer(h, batch, seqlen, od_norm_mix[0], od_w_in[0], fox_b_f[0], od_w_out[0], od_norm_ffn[0],
                   moe_w_router[0], moe_w_gate[0], moe_w_up[0], moe_w_down[0], final_norm)
    return h.reshape(batch, seqlen, d)
```

```python
import functools
import math

import jax
import jax.numpy as jnp
from jax import lax
from jax.experimental import pallas as pl
from jax.experimental.pallas import tpu as pltpu

F32 = jnp.float32
BF16 = jnp.bfloat16

RMS_EPS = 1e-6
LANES = 128
VMEM_LIMIT_BYTES = 56 * 1024 * 1024

S5_GROUP = 16
S5_STATE = 64
S5_CHUNK = 8
SB_HEAD_DIM = 64
FOX_HEAD_DIM = 64
N_EXPERTS = 8
TOP_K = 2
MOE_TILE = 1024
DISPATCH_ROWS = 512
COMBINE_ROWS = 512
MXU_TILE = 256
ATT_BQ = 1024
ATT_BK = 512
LOG2_E = math.log2(math.e)


def _params(*sem):
    return pltpu.CompilerParams(dimension_semantics=sem, vmem_limit_bytes=VMEM_LIMIT_BYTES)


def _rms(x, g):
    ms = jnp.mean(x * x, axis=-1, keepdims=True)
    return x * lax.rsqrt(ms + RMS_EPS) * g


def _dot(a, b):
    return jnp.dot(a, b, preferred_element_type=F32)


_CONTRACT_LAST = (((1,), (1,)), ((), ()))
_CONTRACT_FIRST = (((0,), (0,)), ((), ()))


def _norm_proj_kernel(feature_major, x_ref, g_ref, *refs):
    n_out = len(feature_major)
    w_refs, o_refs = refs[:n_out], refs[n_out:]
    h = _rms(x_ref[...], g_ref[...]).astype(BF16)
    for fm, w_ref, o_ref in zip(feature_major, w_refs, o_refs):
        if fm:
            o_ref[0] = lax.dot_general(w_ref[...], h, _CONTRACT_LAST, preferred_element_type=F32).astype(o_ref.dtype)
        else:
            o_ref[...] = _dot(h, w_ref[...]).astype(o_ref.dtype)


def _norm_proj(x, gain, weights, out_dtypes, feature_major, batch, tm=512):
    t, d = x.shape
    per_seq = t // batch // tm
    in_specs = [pl.BlockSpec((tm, d), lambda i: (i, 0)), pl.BlockSpec((1, d), lambda i: (0, 0))]
    in_specs += [pl.BlockSpec(w.shape, lambda i: (0, 0)) for w in weights]
    out_specs, out_shape = [], []
    for w, dt, fm in zip(weights, out_dtypes, feature_major):
        if fm:
            out_specs.append(pl.BlockSpec((1, w.shape[0], tm), lambda i: (i // per_seq, 0, i % per_seq)))
            out_shape.append(jax.ShapeDtypeStruct((batch, w.shape[0], t // batch), dt))
        else:
            out_specs.append(pl.BlockSpec((tm, w.shape[1]), lambda i: (i, 0)))
            out_shape.append(jax.ShapeDtypeStruct((t, w.shape[1]), dt))
    return pl.pallas_call(
        functools.partial(_norm_proj_kernel, tuple(feature_major)),
        grid=(t // tm,), in_specs=in_specs, out_specs=out_specs, out_shape=out_shape,
        compiler_params=_params("parallel"), name="norm_proj",
    )(x, gain.reshape(1, d), *weights)


def _proj_residual_kernel(feature_major, res_ref, *refs):
    n_in = len(feature_major)
    a_refs, w_refs, o_ref = refs[:n_in], refs[n_in:2 * n_in], refs[2 * n_in]
    acc = res_ref[...]
    for fm, a_ref, w_ref in zip(feature_major, a_refs, w_refs):
        if fm:
            acc = acc + lax.dot_general(a_ref[0], w_ref[...], _CONTRACT_FIRST, preferred_element_type=F32)
        else:
            acc = acc + _dot(a_ref[...], w_ref[...])
    o_ref[...] = acc


def _proj_residual(res, acts, weights, feature_major, batch, tm=512):
    t, n = res.shape
    per_seq = t // batch // tm
    in_specs = [pl.BlockSpec((tm, n), lambda i: (i, 0))]
    for a, fm in zip(acts, feature_major):
        if fm:
            in_specs.append(pl.BlockSpec((1, a.shape[1], tm), lambda i: (i // per_seq, 0, i % per_seq)))
        else:
            in_specs.append(pl.BlockSpec((tm, a.shape[1]), lambda i: (i, 0)))
    in_specs += [pl.BlockSpec(w.shape, lambda i: (0, 0)) for w in weights]
    return pl.pallas_call(
        functools.partial(_proj_residual_kernel, tuple(feature_major)),
        grid=(t // tm,), in_specs=in_specs, out_specs=pl.BlockSpec((tm, n), lambda i: (i, 0)),
        out_shape=jax.ShapeDtypeStruct((t, n), F32),
        compiler_params=_params("parallel"), name="proj_residual",
    )(res, *acts, *weights)


def _s5_tables(a_re, a_im, b_re, b_im, c_re, c_im, log_step, n_scan):
    g, p = a_re.shape
    h = b_re.shape[-1]
    ch = S5_CHUNK
    gl = LANES // h
    nb = g // gl
    lam = lax.complex(a_re, a_im)
    dl = lam * jnp.exp(log_step)[:, None]
    lam_bar = jnp.exp(dl)
    b_bar = ((lam_bar - 1.0) / lam)[..., None] * lax.complex(b_re, b_im)
    c = lax.complex(c_re, c_im)
    steps = jnp.arange(ch + 1, dtype=a_re.dtype)
    pw = jnp.exp(dl[:, None, :] * steps[None, :, None])
    eye = jnp.eye(gl, dtype=a_re.dtype)
    kd = jnp.einsum('gop,gdp,gpi->gdoi', c, pw[:, :ch], b_bar).real
    lag = jnp.arange(ch)[None, :] - jnp.arange(ch)[:, None]
    kfull = kd[:, jnp.clip(lag, 0, ch - 1)]
    kfull = jnp.where((lag >= 0)[None, :, :, None, None], kfull, 0.0)
    kfull = kfull.reshape(nb, gl, ch, ch, h, h)
    k_tab = jnp.einsum('xgstoi,gf->xsgitfo', kfull, eye).reshape(nb, ch * LANES, ch * LANES)
    bc = pw[:, ch - 1 - jnp.arange(ch)][:, :, :, None] * b_bar[:, None]
    bc = jnp.stack([bc.real, bc.imag], axis=2).reshape(nb, gl, ch, 2, p, h)
    b_tab = jnp.einsum('xgscpi,gf->xsgicfp', bc, eye).reshape(nb, ch * LANES, 2 * gl * p)
    m = c[:, None] * pw[:, 1:ch + 1][:, :, None, :]
    m = jnp.stack([m.real, -m.imag], axis=1).reshape(nb, gl, 2, ch, h, p)
    c_tab = jnp.einsum('xgctop,gf->xcgptfo', m, eye).reshape(nb, 2 * gl * p, ch * LANES)
    pows = (ch * (2 ** jnp.arange(n_scan))).astype(a_re.dtype)
    ak = jnp.exp(dl[:, None, :] * pows[None, :, None])
    ak = ak.reshape(nb, gl, n_scan, p).transpose(0, 2, 1, 3).reshape(nb, n_scan, gl * p)
    p1 = jnp.concatenate([ak.real, ak.real], axis=-1)
    p2 = jnp.concatenate([-ak.imag, ak.imag], axis=-1)
    return k_tab.astype(BF16), b_tab.astype(BF16), c_tab.astype(BF16), p1, p2


def _s5_kernel(n_scan, u_ref, k_ref, b_ref, c_ref, p1_ref, p2_ref, y_ref):
    ch = S5_CHUNK
    rows = u_ref.shape[0] // ch
    u = jnp.concatenate([u_ref[pl.ds(t, rows, stride=ch), :].astype(BF16) for t in range(ch)], axis=1)
    state = _dot(u, b_ref[0])
    row = lax.broadcasted_iota(jnp.int32, state.shape, 0)
    half = state.shape[1] // 2
    for k in range(n_scan):
        shift = 1 << k
        prev = jnp.where(row >= shift, pltpu.roll(state, shift, axis=0), 0.0)
        state = state + prev * p1_ref[0, k:k + 1, :] + pltpu.roll(prev, half, axis=1) * p2_ref[0, k:k + 1, :]
    carried = jnp.where(row >= 1, pltpu.roll(state, 1, axis=0), 0.0)
    y = _dot(u, k_ref[0]) + _dot(carried.astype(BF16), c_ref[0])
    for t in range(ch):
        y_ref[pl.ds(t, rows, stride=ch), :] = y[:, t * LANES:(t + 1) * LANES]


def _s5_conv(u, tables, batch, seqlen):
    k_tab, b_tab, c_tab, p1, p2 = tables
    t, width = u.shape
    nb = width // LANES
    n_scan = p1.shape[1]
    kw, sw = k_tab.shape[1], b_tab.shape[2]
    blk = pl.BlockSpec((seqlen, LANES), lambda x, b: (b, x))
    return pl.pallas_call(
        functools.partial(_s5_kernel, n_scan), grid=(nb, batch),
        in_specs=[blk,
                  pl.BlockSpec((1, kw, kw), lambda x, b: (x, 0, 0)),
                  pl.BlockSpec((1, kw, sw), lambda x, b: (x, 0, 0)),
                  pl.BlockSpec((1, sw, kw), lambda x, b: (x, 0, 0)),
                  pl.BlockSpec((1, n_scan, sw), lambda x, b: (x, 0, 0)),
                  pl.BlockSpec((1, n_scan, sw), lambda x, b: (x, 0, 0))],
        out_specs=blk, out_shape=jax.ShapeDtypeStruct((t, width), F32),
        compiler_params=_params("parallel", "parallel"), name="s5_conv",
    )(u, k_tab, b_tab, c_tab, p1, p2)


def _gelu_tanh(x):
    return 0.5 * x * (1.0 + jnp.tanh(math.sqrt(2.0 / math.pi) * (x + 0.044715 * (x * x * x))))


def _s5_glu_kernel(y_ref, u_ref, d_ref, w_ref, b_ref, o_ref):
    y = _gelu_tanh(y_ref[...] + d_ref[...] * u_ref[...])
    gate = jax.nn.sigmoid(_dot(y.astype(BF16), w_ref[...]) + b_ref[...])
    o_ref[...] = (y * gate).astype(o_ref.dtype)


def _s5_glu(y, u, d_skip, w_glu, b_glu, tm=1024):
    t, width = y.shape
    row = pl.BlockSpec((tm, width), lambda i: (i, 0))
    vec = pl.BlockSpec((1, width), lambda i: (0, 0))
    return pl.pallas_call(
        _s5_glu_kernel, grid=(t // tm,),
        in_specs=[row, row, vec, pl.BlockSpec((width, width), lambda i: (0, 0)), vec],
        out_specs=row, out_shape=jax.ShapeDtypeStruct((t, width), BF16),
        compiler_params=_params("parallel"), name="s5_glu",
    )(y, u, d_skip.reshape(1, width), w_glu.astype(BF16), b_glu.reshape(1, width))


def _diag_valid(bk, bq, d, strict):
    key = lax.broadcasted_iota(jnp.int32, (bk, bq), 0) + d * bk
    qry = lax.broadcasted_iota(jnp.int32, (bk, bq), 1)
    return key < qry if strict else key <= qry


def _softplus2(z):
    return jnp.maximum(z, 0.0) + jnp.log2(1.0 + jnp.exp2(-jnp.abs(z)))


def _key_rows(j, bk):
    return pl.ds(pl.multiple_of(j * bk, bk), bk)


def _sb_kernel(bk, qt_ref, k_ref, v_ref, tri_ref, ot_ref):
    qi = pl.program_id(2)
    width, bq = qt_ref.shape[1], qt_ref.shape[2]
    dh = width // 2
    tb = tri_ref.shape[0]
    ratio = bq // bk
    feat = lax.broadcasted_iota(jnp.int32, (width, bq), 0)
    qt2 = qt_ref[0]

    def head(p):
        qt = jnp.where((feat >= p * dh) & (feat < (p + 1) * dh), qt2, jnp.zeros_like(qt2))

        def block(j, carry, acc, diag):
            rows = _key_rows(j, bk)
            z = _dot(k_ref[0, rows, :], qt)
            sp = _softplus2(z)
            if diag is not None:
                valid = _diag_valid(bk, bq, diag, True)
                sp = jnp.where(valid, sp, 0.0)
            sp = sp.astype(BF16)
            parts = []
            for t in reversed(range(bk // tb)):
                sub = slice(t * tb, (t + 1) * tb)
                part = _dot(tri_ref[...], sp[sub]) + carry
                carry = part[0:1, :]
                parts.insert(0, part)
            w = jnp.exp2(z - jnp.concatenate(parts, axis=0))
            if diag is not None:
                w = jnp.where(valid, w, 0.0)
            pv = lax.dot_general(v_ref[0, rows, :], w.astype(BF16), _CONTRACT_FIRST, preferred_element_type=F32)
            return carry, acc + pv

        carry = jnp.zeros((1, bq), F32)
        acc = jnp.zeros((width, bq), F32)
        for d in reversed(range(ratio)):
            carry, acc = block(qi * ratio + d, carry, acc, d)
        carry, acc = lax.fori_loop(0, qi * ratio, lambda i, c: block(qi * ratio - 1 - i, c[0], c[1], None),
                                   (carry, acc))
        return acc

    acc0, acc1 = head(0), head(1)
    ot_ref[0] = jnp.where(feat < dh, acc0, acc1).astype(ot_ref.dtype)


def _sb_attention(qt, kv, bq=ATT_BQ, bk=ATT_BK):
    batch, width, seqlen = qt.shape
    pairs = width // LANES
    tb = MXU_TILE
    tri = (lax.broadcasted_iota(jnp.int32, (tb, tb), 0) <= lax.broadcasted_iota(jnp.int32, (tb, tb), 1)).astype(BF16)
    qspec = pl.BlockSpec((1, LANES, bq), lambda b, h, q: (b, h, q))
    return pl.pallas_call(
        functools.partial(_sb_kernel, bk), grid=(batch, pairs, seqlen // bq),
        in_specs=[qspec,
                  pl.BlockSpec((1, seqlen, LANES), lambda b, h, q: (b, 0, h)),
                  pl.BlockSpec((1, seqlen, LANES), lambda b, h, q: (b, 0, pairs + h)),
                  pl.BlockSpec((tb, tb), lambda b, h, q: (0, 0))],
        out_specs=qspec, out_shape=jax.ShapeDtypeStruct(qt.shape, BF16),
        compiler_params=_params("parallel", "parallel", "arbitrary"), name="sb_attention",
    )(qt, kv, kv, tri)


def _fox_kernel(bk, qt_ref, k_ref, v_ref, cum_ref, ot_ref, bias_ref):
    hp, qi = pl.program_id(1), pl.program_id(2)
    width, bq = qt_ref.shape[1], qt_ref.shape[2]
    dh = width // 2
    ratio = bq // bk
    feat = lax.broadcasted_iota(jnp.int32, (width, bq), 0)
    vlane = lax.broadcasted_iota(jnp.int32, (bk, width), 1)
    qt2 = qt_ref[0]

    @pl.when(qi == 0)
    def _():
        c = cum_ref[0]
        head_lane = lax.broadcasted_iota(jnp.int32, c.shape, 1)
        for p in range(2):
            col = jnp.sum(jnp.where(head_lane == 2 * hp + p, c, 0.0), axis=1, keepdims=True)
            bias_ref[p] = jnp.broadcast_to(-LOG2_E * col, c.shape)

    def head(p):
        mine = (feat >= p * dh) & (feat < (p + 1) * dh)
        qt = jnp.where(mine, qt2, jnp.zeros_like(qt2))
        vmine = (vlane >= p * dh) & (vlane < (p + 1) * dh)

        def block(j, m, acc, diag):
            rows = _key_rows(j, bk)
            bias = bias_ref[p, rows, :]
            s = _dot(k_ref[0, rows, :], qt) + jnp.concatenate([bias] * (bq // LANES), axis=1)
            if diag is not None:
                s = jnp.where(_diag_valid(bk, bq, diag, False), s, -jnp.inf)
            m_new = jnp.maximum(m, jnp.max(s, axis=0, keepdims=True))
            prob = jnp.exp2(s - m_new).astype(BF16)
            v = v_ref[0, rows, :]
            v = jnp.where(vmine, v, jnp.ones_like(v))
            pv = lax.dot_general(v, prob, _CONTRACT_FIRST, preferred_element_type=F32)
            return m_new, jnp.exp2(m - m_new) * acc + pv

        m = jnp.full((1, bq), -jnp.inf, F32)
        acc = jnp.zeros((width, bq), F32)
        m, acc = lax.fori_loop(0, qi * ratio, lambda j, c: block(j, c[0], c[1], None), (m, acc))
        for d in range(ratio):
            m, acc = block(qi * ratio + d, m, acc, d)
        other = (1 - p) * dh
        return acc / acc[other:other + 1]

    out0, out1 = head(0), head(1)
    ot_ref[0] = jnp.where(feat < dh, out0, out1).astype(ot_ref.dtype)


def _fox_attention(qt, kv, cum, bq=ATT_BQ, bk=ATT_BK):
    batch, width, seqlen = qt.shape
    pairs = width // LANES
    qspec = pl.BlockSpec((1, LANES, bq), lambda b, h, q: (b, h, q))
    return pl.pallas_call(
        functools.partial(_fox_kernel, bk), grid=(batch, pairs, seqlen // bq),
        in_specs=[qspec,
                  pl.BlockSpec((1, seqlen, LANES), lambda b, h, q: (b, 0, h)),
                  pl.BlockSpec((1, seqlen, LANES), lambda b, h, q: (b, 0, pairs + h)),
                  pl.BlockSpec((1, seqlen, LANES), lambda b, h, q: (b, 0, 0))],
        out_specs=qspec, out_shape=jax.ShapeDtypeStruct(qt.shape, BF16),
        scratch_shapes=[pltpu.VMEM((2, seqlen, LANES), F32)],
        compiler_params=_params("parallel", "parallel", "arbitrary"), name="fox_attention",
    )(qt, kv, kv, cum)


def _split3(x):
    hi = x.astype(BF16)
    r = x - hi.astype(F32)
    mid = r.astype(BF16)
    lo = (r - mid.astype(F32)).astype(BF16)
    return hi, mid, lo


def _fox_gate_kernel(f_ref, b_ref, tri_ref, c_ref, carry_ref):
    @pl.when(pl.program_id(1) == 0)
    def _():
        carry_ref[...] = jnp.zeros_like(carry_ref)

    x = f_ref[...] + b_ref[...]
    log_f = jnp.minimum(x, 0.0) - jnp.log(1.0 + jnp.exp(-jnp.abs(x)))
    hi, mid, lo = _split3(log_f)
    tri = tri_ref[...]
    c = _dot(tri, hi) + _dot(tri, mid) + _dot(tri, lo) + carry_ref[...]
    c_ref[...] = c
    carry_ref[...] = c[c.shape[0] - 1:, :]


def _fox_gate_cumsum(f_logit, bias, batch, seqlen, tc=512):
    t, w = f_logit.shape
    nchunk = seqlen // tc
    tri = (lax.broadcasted_iota(jnp.int32, (tc, tc), 1) <= lax.broadcasted_iota(jnp.int32, (tc, tc), 0)).astype(BF16)
    return pl.pallas_call(
        _fox_gate_kernel, grid=(batch, nchunk),
        in_specs=[pl.BlockSpec((tc, w), lambda b, c: (b * nchunk + c, 0)),
                  pl.BlockSpec((1, w), lambda b, c: (0, 0)),
                  pl.BlockSpec((tc, tc), lambda b, c: (0, 0))],
        out_specs=pl.BlockSpec((tc, w), lambda b, c: (b * nchunk + c, 0)),
        out_shape=jax.ShapeDtypeStruct((t, w), F32),
        scratch_shapes=[pltpu.VMEM((1, w), F32)],
        compiler_params=_params("parallel", "arbitrary"), name="fox_gate_cumsum",
    )(f_logit, bias, tri)


def _router_kernel(x_ref, g_ref, whi_ref, wlo_ref, o_ref):
    h = _rms(x_ref[...], g_ref[...])
    h_hi = h.astype(BF16)
    h_lo = (h - h_hi.astype(F32)).astype(BF16)
    logits = _dot(h_hi, whi_ref[...]) + _dot(h_lo, whi_ref[...]) + _dot(h_hi, wlo_ref[...])
    lane = lax.broadcasted_iota(jnp.int32, logits.shape, 1)
    logits = jnp.where(lane < N_EXPERTS, logits, -jnp.inf)
    m1 = jnp.max(logits, axis=1, keepdims=True)
    i1 = jnp.min(jnp.where(logits == m1, lane, LANES), axis=1, keepdims=True)
    rest = jnp.where(lane == i1, -jnp.inf, logits)
    m2 = jnp.max(rest, axis=1, keepdims=True)
    i2 = jnp.min(jnp.where(rest == m2, lane, LANES), axis=1, keepdims=True)
    e2 = jnp.exp(m2 - m1)
    g1 = 1.0 / (1.0 + e2)
    out = jnp.where(lane == 0, g1, 0.0) + jnp.where(lane == 1, e2 * g1, 0.0)
    out = out + jnp.where(lane == 2, i1.astype(F32), 0.0) + jnp.where(lane == 3, i2.astype(F32), 0.0)
    o_ref[...] = out


def _router(x, gain, w_router, tm=1024):
    t, d = x.shape
    w = jnp.zeros((d, LANES), F32).at[:, :N_EXPERTS].set(w_router)
    w_hi = w.astype(BF16)
    w_lo = (w - w_hi.astype(F32)).astype(BF16)
    return pl.pallas_call(
        _router_kernel, grid=(t // tm,),
        in_specs=[pl.BlockSpec((tm, d), lambda i: (i, 0)), pl.BlockSpec((1, d), lambda i: (0, 0)),
                  pl.BlockSpec((d, LANES), lambda i: (0, 0)), pl.BlockSpec((d, LANES), lambda i: (0, 0))],
        out_specs=pl.BlockSpec((tm, LANES), lambda i: (i, 0)),
        out_shape=jax.ShapeDtypeStruct((t, LANES), F32),
        compiler_params=_params("parallel"), name="router",
    )(x, gain.reshape(1, d), w_hi, w_lo)


def _route_plan(route, tm):
    t = route.shape[0]
    n = TOP_K * t
    n_tiles = n // tm
    expert = route[:, 2:2 + TOP_K].astype(jnp.int32).T.reshape(-1)
    onehot = (expert[:, None] == jnp.arange(N_EXPERTS, dtype=jnp.int32)[None, :]).astype(jnp.int32)
    rank = jnp.sum((jnp.cumsum(onehot, axis=0) - onehot) * onehot, axis=1)
    counts = jnp.sum(onehot, axis=0)
    ends = jnp.cumsum(counts)
    pos = (ends - counts)[expert] + rank
    tile_start = jnp.arange(n_tiles, dtype=jnp.int32) * tm
    lo = jnp.sort(jnp.concatenate([tile_start, ends[:-1].astype(jnp.int32)]))
    hi = jnp.concatenate([lo[1:], jnp.full((1,), n, jnp.int32)])
    tile = jnp.minimum(lo // tm, n_tiles - 1)
    exp_id = jnp.minimum(jnp.sum((lo[:, None] >= ends[None, :]).astype(jnp.int32), axis=1), N_EXPERTS - 1)
    first = jnp.concatenate([jnp.ones((1,), jnp.int32), (tile[1:] != tile[:-1]).astype(jnp.int32)])
    last = jnp.concatenate([(tile[1:] != tile[:-1]).astype(jnp.int32), jnp.ones((1,), jnp.int32)])
    visits = jnp.stack([tile, exp_id, lo, hi, first + 2 * last]).astype(jnp.int32)
    return pos.reshape(TOP_K, t).astype(jnp.int32), visits


def _slot_indices(pos, rows):
    t = pos.shape[1]
    return pos.reshape(TOP_K, t // rows, rows).transpose(1, 0, 2).reshape(t // rows, TOP_K * rows)


def _dispatch_kernel(rows, idx_hbm, x_ref, xs_hbm, idx_smem, idx_sem, row_sem):
    idx_copy = pltpu.make_async_copy(idx_hbm.at[pl.program_id(0)], idx_smem, idx_sem)
    idx_copy.start()
    idx_copy.wait()

    def issue(r, _):
        for k in range(TOP_K):
            pltpu.make_async_copy(x_ref.at[pl.ds(r, 1)], xs_hbm.at[pl.ds(idx_smem[k * rows + r], 1)], row_sem).start()
        return 0

    lax.fori_loop(0, rows, issue, 0, unroll=8)
    done = xs_hbm.at[pl.ds(0, TOP_K * rows)]
    pltpu.make_async_copy(done, done, row_sem).wait()


def _dispatch(x, pos, rows=DISPATCH_ROWS):
    t, d = x.shape
    return pl.pallas_call(
        functools.partial(_dispatch_kernel, rows), grid=(t // rows,),
        in_specs=[pl.BlockSpec(memory_space=pl.ANY), pl.BlockSpec((rows, d), lambda i: (i, 0))],
        out_specs=pl.BlockSpec(memory_space=pl.ANY),
        out_shape=jax.ShapeDtypeStruct((TOP_K * t, d), x.dtype),
        scratch_shapes=[pltpu.SMEM((TOP_K * rows,), jnp.int32), pltpu.SemaphoreType.DMA(()),
                        pltpu.SemaphoreType.DMA(())],
        compiler_params=_params("arbitrary"), name="moe_dispatch",
    )(_slot_indices(pos, rows), x)


def _experts_kernel(v_ref, x_ref, g_ref, wg_ref, wu_ref, wd_ref, o_ref, h_ref, acc_ref):
    v, f = pl.program_id(0), pl.program_id(1)
    tm = x_ref.shape[0]
    lo, hi, flags = v_ref[2, v], v_ref[3, v], v_ref[4, v]

    @pl.when(jnp.logical_and((flags & 1) == 1, f == 0))
    def _():
        h_ref[...] = _rms(x_ref[...], g_ref[...]).astype(BF16)
        acc_ref[...] = jnp.zeros_like(acc_ref)

    @pl.when(hi > lo)
    def _():
        h = h_ref[...]
        gate = _dot(h, wg_ref[0])
        up = _dot(h, wu_ref[0])
        row = lax.broadcasted_iota(jnp.int32, (tm, 1), 0) + v_ref[0, v] * tm
        mine = jnp.logical_and(row >= lo, row < hi)
        act = jnp.where(mine, gate * jax.nn.sigmoid(gate) * up, 0.0)
        acc_ref[...] += _dot(act.astype(BF16), wd_ref[0])

    @pl.when(jnp.logical_and((flags & 2) == 2, f == pl.num_programs(1) - 1))
    def _():
        o_ref[...] = acc_ref[...]


def _experts(xs, gain, visits, w_gate, w_up, w_down, tm, tf=512):
    n, d = xs.shape
    ff = w_gate.shape[2]

    def row(v, f, vis):
        return vis[0, v], 0

    grid_spec = pltpu.PrefetchScalarGridSpec(
        num_scalar_prefetch=1, grid=(visits.shape[1], ff // tf),
        in_specs=[pl.BlockSpec((tm, d), row), pl.BlockSpec((1, d), lambda v, f, vis: (0, 0)),
                  pl.BlockSpec((1, d, tf), lambda v, f, vis: (vis[1, v], 0, f)),
                  pl.BlockSpec((1, d, tf), lambda v, f, vis: (vis[1, v], 0, f)),
                  pl.BlockSpec((1, tf, d), lambda v, f, vis: (vis[1, v], f, 0))],
        out_specs=pl.BlockSpec((tm, d), row),
        scratch_shapes=[pltpu.VMEM((tm, d), BF16), pltpu.VMEM((tm, d), F32)])
    return pl.pallas_call(
        _experts_kernel, grid_spec=grid_spec, out_shape=jax.ShapeDtypeStruct((n, d), F32),
        compiler_params=_params("arbitrary", "arbitrary"), name="experts",
    )(visits, xs, gain.reshape(1, d), w_gate, w_up, w_down)


def _combine_kernel(rows, idx_hbm, x_ref, route_ref, g_ref, y_hbm, o_ref, idx_smem, buf, idx_sem, row_sem):
    idx_copy = pltpu.make_async_copy(idx_hbm.at[pl.program_id(0)], idx_smem, idx_sem)
    idx_copy.start()
    idx_copy.wait()

    def issue(r, _):
        for k in range(TOP_K):
            pltpu.make_async_copy(y_hbm.at[pl.ds(idx_smem[k * rows + r], 1)], buf.at[k, pl.ds(r, 1)], row_sem).start()
        return 0

    lax.fori_loop(0, rows, issue, 0, unroll=8)
    pltpu.make_async_copy(buf, buf, row_sem).wait()
    out = x_ref[...]
    for k in range(TOP_K):
        out = out + route_ref[:, k:k + 1] * buf[k]
    o_ref[...] = _rms(out, g_ref[...])


def _combine(x, y, pos, route, gain, rows=COMBINE_ROWS):
    t, d = x.shape
    return pl.pallas_call(
        functools.partial(_combine_kernel, rows), grid=(t // rows,),
        in_specs=[pl.BlockSpec(memory_space=pl.ANY), pl.BlockSpec((rows, d), lambda i: (i, 0)),
                  pl.BlockSpec((rows, LANES), lambda i: (i, 0)), pl.BlockSpec((1, d), lambda i: (0, 0)),
                  pl.BlockSpec(memory_space=pl.ANY)],
        out_specs=pl.BlockSpec((rows, d), lambda i: (i, 0)),
        out_shape=jax.ShapeDtypeStruct((t, d), F32),
        scratch_shapes=[pltpu.SMEM((TOP_K * rows,), jnp.int32), pltpu.VMEM((TOP_K, rows, d), F32),
                        pltpu.SemaphoreType.DMA(()), pltpu.SemaphoreType.DMA(())],
        compiler_params=_params("arbitrary"), name="moe_combine",
    )(_slot_indices(pos, rows), x, route, gain.reshape(1, d), y)


def _swiglu_kernel(x_ref, g_ref, wg_ref, wu_ref, wd_ref, o_ref, h_ref, acc_ref):
    f = pl.program_id(1)

    @pl.when(f == 0)
    def _():
        h_ref[...] = _rms(x_ref[...], g_ref[...]).astype(BF16)
        acc_ref[...] = jnp.zeros_like(acc_ref)

    h = h_ref[...]
    gate = _dot(h, wg_ref[...])
    up = _dot(h, wu_ref[...])
    act = gate * jax.nn.sigmoid(gate) * up
    acc_ref[...] += _dot(act.astype(BF16), wd_ref[...])

    @pl.when(f == pl.num_programs(1) - 1)
    def _():
        o_ref[...] = x_ref[...] + acc_ref[...]


def _swiglu(x, gain, w_gate, w_up, w_down, tm=1024, tf=256):
    t, d = x.shape
    ff = w_gate.shape[1]
    row = pl.BlockSpec((tm, d), lambda i, f: (i, 0))
    return pl.pallas_call(
        _swiglu_kernel, grid=(t // tm, ff // tf),
        in_specs=[row, pl.BlockSpec((1, d), lambda i, f: (0, 0)),
                  pl.BlockSpec((d, tf), lambda i, f: (0, f)), pl.BlockSpec((d, tf), lambda i, f: (0, f)),
                  pl.BlockSpec((tf, d), lambda i, f: (f, 0))],
        out_specs=row, out_shape=jax.ShapeDtypeStruct((t, d), F32),
        scratch_shapes=[pltpu.VMEM((tm, d), BF16), pltpu.VMEM((tm, d), F32)],
        compiler_params=_params("parallel", "arbitrary"), name="swiglu",
    )(x, gain.reshape(1, d), w_gate, w_up, w_down)


def _even_layer(x, batch, seqlen, norm_mix, w_in, s5, w_glu, b_glu, d_skip, w_out, norm_ffn, w_gate, w_up, w_down):
    s5_width = d_skip.size
    sb_width = (w_in.shape[1] - s5_width) // 3
    scale = LOG2_E / math.sqrt(SB_HEAD_DIM)
    w_u = w_in[:, :s5_width].astype(BF16)
    w_q = (w_in[:, s5_width:s5_width + sb_width] * scale).astype(BF16)
    w_kv = w_in[:, s5_width + sb_width:].astype(BF16)
    u, qt, kv = _norm_proj(x, norm_mix, [w_u, w_q.T, w_kv], [F32, BF16, BF16], [False, True, False], batch)

    n_scan = int(math.log2(seqlen // S5_CHUNK))
    tables = _s5_tables(*s5, n_scan)
    y = _s5_conv(u, tables, batch, seqlen)
    a_out = _s5_glu(y, u, d_skip, w_glu, b_glu)

    b_out_t = _sb_attention(qt, kv.reshape(batch, seqlen, 2 * sb_width))

    w_out = w_out.astype(BF16)
    x = _proj_residual(x, [a_out, b_out_t], [w_out[:s5_width], w_out[s5_width:]], [False, True], batch)
    return _swiglu(x, norm_ffn, w_gate.astype(BF16), w_up.astype(BF16), w_down.astype(BF16))


def _odd_layer(x, batch, seqlen, norm_mix, w_in, b_f, w_out, norm_ffn, w_router, w_gate, w_up, w_down, final_gain):
    heads = b_f.size
    width = heads * FOX_HEAD_DIM
    scale = LOG2_E / math.sqrt(FOX_HEAD_DIM)
    d = x.shape[1]
    w_q = (w_in[:, :width] * scale).astype(BF16)
    w_kv = w_in[:, width:3 * width].astype(BF16)
    w_f = jnp.zeros((d, LANES), F32).at[:, :heads].set(w_in[:, 3 * width:]).astype(BF16)
    qt, kv, f_logit = _norm_proj(x, norm_mix, [w_q.T, w_kv, w_f], [BF16, BF16, F32], [True, False, False], batch)
    bias = jnp.zeros((1, LANES), F32).at[0, :heads].set(b_f)
    cum = _fox_gate_cumsum(f_logit, bias, batch, seqlen)

    c_out_t = _fox_attention(qt, kv.reshape(batch, seqlen, 2 * width), cum.reshape(batch, seqlen, LANES))
    x = _proj_residual(x, [c_out_t], [w_out.astype(BF16)], [True], batch)
    route = _router(x, norm_ffn, w_router)
    pos, visits = _route_plan(route, MOE_TILE)
    xs = _dispatch(x, pos)
    ys = _experts(xs, norm_ffn, visits, w_gate.astype(BF16), w_up.astype(BF16), w_down.astype(BF16), MOE_TILE)
    return _combine(x, ys, pos, route, final_gain)


def kernel(x, ev_norm_mix, ev_w_in, s5_a_re, s5_a_im, s5_b_re, s5_b_im, s5_c_re, s5_c_im, s5_d, s5_log_step,
           s5_w_glu, s5_b_glu, ev_w_out, ev_norm_ffn, ffn_w_gate, ffn_w_up, ffn_w_down, od_norm_mix, od_w_in,
           fox_b_f, od_w_out, od_norm_ffn, moe_w_router, moe_w_gate, moe_w_up, moe_w_down, final_norm):
    batch, seqlen, d = x.shape
    assert ev_w_in.shape[0] == 1 and od_w_in.shape[0] == 1, "one even and one odd layer"
    h = x.reshape(batch * seqlen, d)
    s5 = (s5_a_re[0], s5_a_im[0], s5_b_re[0], s5_b_im[0], s5_c_re[0], s5_c_im[0], s5_log_step[0])
    h = _even_layer(h, batch, seqlen, ev_norm_mix[0], ev_w_in[0], s5, s5_w_glu[0], s5_b_glu[0], s5_d[0].reshape(-1),
                    ev_w_out[0], ev_norm_ffn[0], ffn_w_gate[0], ffn_w_up[0], ffn_w_down[0])
    h = _odd_layer(h, batch, seqlen, od_norm_mix[0], od_w_in[0], fox_b_f[0], od_w_out[0], od_norm_ffn[0],
                   moe_w_router[0], moe_w_gate[0], moe_w_up[0], moe_w_down[0], final_norm)
    return h.reshape(batch, seqlen, d)
```

```python
import functools
import math

import jax
import jax.numpy as jnp
from jax import lax
from jax.experimental import pallas as pl
from jax.experimental.pallas import tpu as pltpu

F32 = jnp.float32
BF16 = jnp.bfloat16

RMS_EPS = 1e-6
LANES = 128
VMEM_LIMIT_BYTES = 56 * 1024 * 1024

S5_GROUP = 16
S5_STATE = 64
S5_CHUNK = 8
SB_HEAD_DIM = 64
FOX_HEAD_DIM = 64
N_EXPERTS = 8
TOP_K = 2
MOE_TILE = 1024
DISPATCH_ROWS = 512
COMBINE_ROWS = 512
MXU_TILE = 256
ATT_BQ = 1024
ATT_BK = 1024
LOG2_E = math.log2(math.e)


def _params(*sem):
    return pltpu.CompilerParams(dimension_semantics=sem, vmem_limit_bytes=VMEM_LIMIT_BYTES)


def _rms(x, g):
    ms = jnp.mean(x * x, axis=-1, keepdims=True)
    return x * lax.rsqrt(ms + RMS_EPS) * g


def _dot(a, b):
    return jnp.dot(a, b, preferred_element_type=F32)


_CONTRACT_LAST = (((1,), (1,)), ((), ()))
_CONTRACT_FIRST = (((0,), (0,)), ((), ()))


def _norm_proj_kernel(feature_major, x_ref, g_ref, *refs):
    n_out = len(feature_major)
    w_refs, o_refs = refs[:n_out], refs[n_out:]
    h = _rms(x_ref[...], g_ref[...]).astype(BF16)
    for fm, w_ref, o_ref in zip(feature_major, w_refs, o_refs):
        if fm:
            o_ref[0] = lax.dot_general(w_ref[...], h, _CONTRACT_LAST, preferred_element_type=F32).astype(o_ref.dtype)
        else:
            o_ref[...] = _dot(h, w_ref[...]).astype(o_ref.dtype)


def _norm_proj(x, gain, weights, out_dtypes, feature_major, batch, tm=512):
    t, d = x.shape
    per_seq = t // batch // tm
    in_specs = [pl.BlockSpec((tm, d), lambda i: (i, 0)), pl.BlockSpec((1, d), lambda i: (0, 0))]
    in_specs += [pl.BlockSpec(w.shape, lambda i: (0, 0)) for w in weights]
    out_specs, out_shape = [], []
    for w, dt, fm in zip(weights, out_dtypes, feature_major):
        if fm:
            out_specs.append(pl.BlockSpec((1, w.shape[0], tm), lambda i: (i // per_seq, 0, i % per_seq)))
            out_shape.append(jax.ShapeDtypeStruct((batch, w.shape[0], t // batch), dt))
        else:
            out_specs.append(pl.BlockSpec((tm, w.shape[1]), lambda i: (i, 0)))
            out_shape.append(jax.ShapeDtypeStruct((t, w.shape[1]), dt))
    return pl.pallas_call(
        functools.partial(_norm_proj_kernel, tuple(feature_major)),
        grid=(t // tm,), in_specs=in_specs, out_specs=out_specs, out_shape=out_shape,
        compiler_params=_params("parallel"), name="norm_proj",
    )(x, gain.reshape(1, d), *weights)


def _proj_residual_kernel(feature_major, res_ref, *refs):
    n_in = len(feature_major)
    a_refs, w_refs, o_ref = refs[:n_in], refs[n_in:2 * n_in], refs[2 * n_in]
    acc = res_ref[...]
    for fm, a_ref, w_ref in zip(feature_major, a_refs, w_refs):
        if fm:
            acc = acc + lax.dot_general(a_ref[0], w_ref[...], _CONTRACT_FIRST, preferred_element_type=F32)
        else:
            acc = acc + _dot(a_ref[...], w_ref[...])
    o_ref[...] = acc


def _proj_residual(res, acts, weights, feature_major, batch, tm=512):
    t, n = res.shape
    per_seq = t // batch // tm
    in_specs = [pl.BlockSpec((tm, n), lambda i: (i, 0))]
    for a, fm in zip(acts, feature_major):
        if fm:
            in_specs.append(pl.BlockSpec((1, a.shape[1], tm), lambda i: (i // per_seq, 0, i % per_seq)))
        else:
            in_specs.append(pl.BlockSpec((tm, a.shape[1]), lambda i: (i, 0)))
    in_specs += [pl.BlockSpec(w.shape, lambda i: (0, 0)) for w in weights]
    return pl.pallas_call(
        functools.partial(_proj_residual_kernel, tuple(feature_major)),
        grid=(t // tm,), in_specs=in_specs, out_specs=pl.BlockSpec((tm, n), lambda i: (i, 0)),
        out_shape=jax.ShapeDtypeStruct((t, n), F32),
        compiler_params=_params("parallel"), name="proj_residual",
    )(res, *acts, *weights)


def _s5_tables(a_re, a_im, b_re, b_im, c_re, c_im, log_step, n_scan):
    g, p = a_re.shape
    h = b_re.shape[-1]
    ch = S5_CHUNK
    gl = LANES // h
    nb = g // gl
    lam = lax.complex(a_re, a_im)
    dl = lam * jnp.exp(log_step)[:, None]
    lam_bar = jnp.exp(dl)
    b_bar = ((lam_bar - 1.0) / lam)[..., None] * lax.complex(b_re, b_im)
    c = lax.complex(c_re, c_im)
    steps = jnp.arange(ch + 1, dtype=a_re.dtype)
    pw = jnp.exp(dl[:, None, :] * steps[None, :, None])
    eye = jnp.eye(gl, dtype=a_re.dtype)
    kd = jnp.einsum('gop,gdp,gpi->gdoi', c, pw[:, :ch], b_bar).real
    lag = jnp.arange(ch)[None, :] - jnp.arange(ch)[:, None]
    kfull = kd[:, jnp.clip(lag, 0, ch - 1)]
    kfull = jnp.where((lag >= 0)[None, :, :, None, None], kfull, 0.0)
    kfull = kfull.reshape(nb, gl, ch, ch, h, h)
    k_tab = jnp.einsum('xgstoi,gf->xsgitfo', kfull, eye).reshape(nb, ch * LANES, ch * LANES)
    bc = pw[:, ch - 1 - jnp.arange(ch)][:, :, :, None] * b_bar[:, None]
    bc = jnp.stack([bc.real, bc.imag], axis=2).reshape(nb, gl, ch, 2, p, h)
    b_tab = jnp.einsum('xgscpi,gf->xsgicfp', bc, eye).reshape(nb, ch * LANES, 2 * gl * p)
    m = c[:, None] * pw[:, 1:ch + 1][:, :, None, :]
    m = jnp.stack([m.real, -m.imag], axis=1).reshape(nb, gl, 2, ch, h, p)
    c_tab = jnp.einsum('xgctop,gf->xcgptfo', m, eye).reshape(nb, 2 * gl * p, ch * LANES)
    pows = (ch * (2 ** jnp.arange(n_scan))).astype(a_re.dtype)
    ak = jnp.exp(dl[:, None, :] * pows[None, :, None])
    ak = ak.reshape(nb, gl, n_scan, p).transpose(0, 2, 1, 3).reshape(nb, n_scan, gl * p)
    p1 = jnp.concatenate([ak.real, ak.real], axis=-1)
    p2 = jnp.concatenate([-ak.imag, ak.imag], axis=-1)
    return k_tab.astype(BF16), b_tab.astype(BF16), c_tab.astype(BF16), p1, p2


def _s5_kernel(n_scan, u_ref, k_ref, b_ref, c_ref, p1_ref, p2_ref, y_ref):
    ch = S5_CHUNK
    rows = u_ref.shape[0] // ch
    u = jnp.concatenate([u_ref[pl.ds(t, rows, stride=ch), :].astype(BF16) for t in range(ch)], axis=1)
    state = _dot(u, b_ref[0])
    row = lax.broadcasted_iota(jnp.int32, state.shape, 0)
    half = state.shape[1] // 2
    for k in range(n_scan):
        shift = 1 << k
        prev = jnp.where(row >= shift, pltpu.roll(state, shift, axis=0), 0.0)
        state = state + prev * p1_ref[0, k:k + 1, :] + pltpu.roll(prev, half, axis=1) * p2_ref[0, k:k + 1, :]
    carried = jnp.where(row >= 1, pltpu.roll(state, 1, axis=0), 0.0)
    y = _dot(u, k_ref[0]) + _dot(carried.astype(BF16), c_ref[0])
    for t in range(ch):
        y_ref[pl.ds(t, rows, stride=ch), :] = y[:, t * LANES:(t + 1) * LANES]


def _s5_conv(u, tables, batch, seqlen):
    k_tab, b_tab, c_tab, p1, p2 = tables
    t, width = u.shape
    nb = width // LANES
    n_scan = p1.shape[1]
    kw, sw = k_tab.shape[1], b_tab.shape[2]
    blk = pl.BlockSpec((seqlen, LANES), lambda x, b: (b, x))
    return pl.pallas_call(
        functools.partial(_s5_kernel, n_scan), grid=(nb, batch),
        in_specs=[blk,
                  pl.BlockSpec((1, kw, kw), lambda x, b: (x, 0, 0)),
                  pl.BlockSpec((1, kw, sw), lambda x, b: (x, 0, 0)),
                  pl.BlockSpec((1, sw, kw), lambda x, b: (x, 0, 0)),
                  pl.BlockSpec((1, n_scan, sw), lambda x, b: (x, 0, 0)),
                  pl.BlockSpec((1, n_scan, sw), lambda x, b: (x, 0, 0))],
        out_specs=blk, out_shape=jax.ShapeDtypeStruct((t, width), F32),
        compiler_params=_params("parallel", "parallel"), name="s5_conv",
    )(u, k_tab, b_tab, c_tab, p1, p2)


def _gelu_tanh(x):
    return 0.5 * x * (1.0 + jnp.tanh(math.sqrt(2.0 / math.pi) * (x + 0.044715 * (x * x * x))))


def _s5_glu_kernel(y_ref, u_ref, d_ref, w_ref, b_ref, o_ref):
    y = _gelu_tanh(y_ref[...] + d_ref[...] * u_ref[...])
    gate = jax.nn.sigmoid(_dot(y.astype(BF16), w_ref[...]) + b_ref[...])
    o_ref[...] = (y * gate).astype(o_ref.dtype)


def _s5_glu(y, u, d_skip, w_glu, b_glu, tm=1024):
    t, width = y.shape
    row = pl.BlockSpec((tm, width), lambda i: (i, 0))
    vec = pl.BlockSpec((1, width), lambda i: (0, 0))
    return pl.pallas_call(
        _s5_glu_kernel, grid=(t // tm,),
        in_specs=[row, row, vec, pl.BlockSpec((width, width), lambda i: (0, 0)), vec],
        out_specs=row, out_shape=jax.ShapeDtypeStruct((t, width), BF16),
        compiler_params=_params("parallel"), name="s5_glu",
    )(y, u, d_skip.reshape(1, width), w_glu.astype(BF16), b_glu.reshape(1, width))


def _diag_valid(bk, bq, d, strict):
    key = lax.broadcasted_iota(jnp.int32, (bk, bq), 0) + d * bk
    qry = lax.broadcasted_iota(jnp.int32, (bk, bq), 1)
    return key < qry if strict else key <= qry


def _softplus2(z):
    return jnp.maximum(z, 0.0) + jnp.log2(1.0 + jnp.exp2(-jnp.abs(z)))


def _key_rows(j, bk):
    return pl.ds(pl.multiple_of(j * bk, bk), bk)


def _sb_kernel(bk, qt_ref, k_ref, v_ref, tri_ref, ot_ref):
    qi = pl.program_id(2)
    width, bq = qt_ref.shape[1], qt_ref.shape[2]
    dh = width // 2
    tb = tri_ref.shape[0]
    ratio = bq // bk
    feat = lax.broadcasted_iota(jnp.int32, (width, bq), 0)
    qt2 = qt_ref[0]

    def head(p):
        qt = jnp.where((feat >= p * dh) & (feat < (p + 1) * dh), qt2, jnp.zeros_like(qt2))

        def block(j, carry, acc, diag):
            rows = _key_rows(j, bk)
            z = _dot(k_ref[0, rows, :], qt)
            sp = _softplus2(z)
            if diag is not None:
                valid = _diag_valid(bk, bq, diag, True)
                sp = jnp.where(valid, sp, 0.0)
            sp = sp.astype(BF16)
            parts = []
            for t in reversed(range(bk // tb)):
                sub = slice(t * tb, (t + 1) * tb)
                part = _dot(tri_ref[...], sp[sub]) + carry
                carry = part[0:1, :]
                parts.insert(0, part)
            w = jnp.exp2(z - jnp.concatenate(parts, axis=0))
            if diag is not None:
                w = jnp.where(valid, w, 0.0)
            pv = lax.dot_general(v_ref[0, rows, :], w.astype(BF16), _CONTRACT_FIRST, preferred_element_type=F32)
            return carry, acc + pv

        carry = jnp.zeros((1, bq), F32)
        acc = jnp.zeros((width, bq), F32)
        for d in reversed(range(ratio)):
            carry, acc = block(qi * ratio + d, carry, acc, d)
        carry, acc = lax.fori_loop(0, qi * ratio, lambda i, c: block(qi * ratio - 1 - i, c[0], c[1], None),
                                   (carry, acc))
        return acc

    acc0, acc1 = head(0), head(1)
    ot_ref[0] = jnp.where(feat < dh, acc0, acc1).astype(ot_ref.dtype)


def _sb_attention(qt, kv, bq=ATT_BQ, bk=ATT_BK):
    batch, width, seqlen = qt.shape
    pairs = width // LANES
    tb = MXU_TILE
    tri = (lax.broadcasted_iota(jnp.int32, (tb, tb), 0) <= lax.broadcasted_iota(jnp.int32, (tb, tb), 1)).astype(BF16)
    qspec = pl.BlockSpec((1, LANES, bq), lambda b, h, q: (b, h, q))
    return pl.pallas_call(
        functools.partial(_sb_kernel, bk), grid=(batch, pairs, seqlen // bq),
        in_specs=[qspec,
                  pl.BlockSpec((1, seqlen, LANES), lambda b, h, q: (b, 0, h)),
                  pl.BlockSpec((1, seqlen, LANES), lambda b, h, q: (b, 0, pairs + h)),
                  pl.BlockSpec((tb, tb), lambda b, h, q: (0, 0))],
        out_specs=qspec, out_shape=jax.ShapeDtypeStruct(qt.shape, BF16),
        compiler_params=_params("parallel", "parallel", "arbitrary"), name="sb_attention",
    )(qt, kv, kv, tri)


def _fox_kernel(bk, qt_ref, k_ref, v_ref, cum_ref, ot_ref, bias_ref):
    hp, qi = pl.program_id(1), pl.program_id(2)
    width, bq = qt_ref.shape[1], qt_ref.shape[2]
    dh = width // 2
    ratio = bq // bk
    feat = lax.broadcasted_iota(jnp.int32, (width, bq), 0)
    vlane = lax.broadcasted_iota(jnp.int32, (bk, width), 1)
    qt2 = qt_ref[0]

    @pl.when(qi == 0)
    def _():
        c = cum_ref[0]
        head_lane = lax.broadcasted_iota(jnp.int32, c.shape, 1)
        for p in range(2):
            col = jnp.sum(jnp.where(head_lane == 2 * hp + p, c, 0.0), axis=1, keepdims=True)
            bias_ref[p] = jnp.broadcast_to(-LOG2_E * col, c.shape)

    def head(p):
        mine = (feat >= p * dh) & (feat < (p + 1) * dh)
        qt = jnp.where(mine, qt2, jnp.zeros_like(qt2))
        vmine = (vlane >= p * dh) & (vlane < (p + 1) * dh)

        def block(j, m, acc, diag):
            rows = _key_rows(j, bk)
            bias = bias_ref[p, rows, :]
            s = _dot(k_ref[0, rows, :], qt) + jnp.concatenate([bias] * (bq // LANES), axis=1)
            if diag is not None:
                s = jnp.where(_diag_valid(bk, bq, diag, False), s, -jnp.inf)
            m_new = jnp.maximum(m, jnp.max(s, axis=0, keepdims=True))
            prob = jnp.exp2(s - m_new).astype(BF16)
            v = v_ref[0, rows, :]
            v = jnp.where(vmine, v, jnp.ones_like(v))
            pv = lax.dot_general(v, prob, _CONTRACT_FIRST, preferred_element_type=F32)
            return m_new, jnp.exp2(m - m_new) * acc + pv

        m = jnp.full((1, bq), -jnp.inf, F32)
        acc = jnp.zeros((width, bq), F32)
        m, acc = lax.fori_loop(0, qi * ratio, lambda j, c: block(j, c[0], c[1], None), (m, acc))
        for d in range(ratio):
            m, acc = block(qi * ratio + d, m, acc, d)
        other = (1 - p) * dh
        return acc / acc[other:other + 1]

    out0, out1 = head(0), head(1)
    ot_ref[0] = jnp.where(feat < dh, out0, out1).astype(ot_ref.dtype)


def _fox_attention(qt, kv, cum, bq=ATT_BQ, bk=ATT_BK):
    batch, width, seqlen = qt.shape
    pairs = width // LANES
    qspec = pl.BlockSpec((1, LANES, bq), lambda b, h, q: (b, h, q))
    return pl.pallas_call(
        functools.partial(_fox_kernel, bk), grid=(batch, pairs, seqlen // bq),
        in_specs=[qspec,
                  pl.BlockSpec((1, seqlen, LANES), lambda b, h, q: (b, 0, h)),
                  pl.BlockSpec((1, seqlen, LANES), lambda b, h, q: (b, 0, pairs + h)),
                  pl.BlockSpec((1, seqlen, LANES), lambda b, h, q: (b, 0, 0))],
        out_specs=qspec, out_shape=jax.ShapeDtypeStruct(qt.shape, BF16),
        scratch_shapes=[pltpu.VMEM((2, seqlen, LANES), F32)],
        compiler_params=_params("parallel", "parallel", "arbitrary"), name="fox_attention",
    )(qt, kv, kv, cum)


def _split3(x):
    hi = x.astype(BF16)
    r = x - hi.astype(F32)
    mid = r.astype(BF16)
    lo = (r - mid.astype(F32)).astype(BF16)
    return hi, mid, lo


def _fox_gate_kernel(f_ref, b_ref, tri_ref, c_ref, carry_ref):
    @pl.when(pl.program_id(1) == 0)
    def _():
        carry_ref[...] = jnp.zeros_like(carry_ref)

    x = f_ref[...] + b_ref[...]
    log_f = jnp.minimum(x, 0.0) - jnp.log(1.0 + jnp.exp(-jnp.abs(x)))
    hi, mid, lo = _split3(log_f)
    tri = tri_ref[...]
    c = _dot(tri, hi) + _dot(tri, mid) + _dot(tri, lo) + carry_ref[...]
    c_ref[...] = c
    carry_ref[...] = c[c.shape[0] - 1:, :]


def _fox_gate_cumsum(f_logit, bias, batch, seqlen, tc=512):
    t, w = f_logit.shape
    nchunk = seqlen // tc
    tri = (lax.broadcasted_iota(jnp.int32, (tc, tc), 1) <= lax.broadcasted_iota(jnp.int32, (tc, tc), 0)).astype(BF16)
    return pl.pallas_call(
        _fox_gate_kernel, grid=(batch, nchunk),
        in_specs=[pl.BlockSpec((tc, w), lambda b, c: (b * nchunk + c, 0)),
                  pl.BlockSpec((1, w), lambda b, c: (0, 0)),
                  pl.BlockSpec((tc, tc), lambda b, c: (0, 0))],
        out_specs=pl.BlockSpec((tc, w), lambda b, c: (b * nchunk + c, 0)),
        out_shape=jax.ShapeDtypeStruct((t, w), F32),
        scratch_shapes=[pltpu.VMEM((1, w), F32)],
        compiler_params=_params("parallel", "arbitrary"), name="fox_gate_cumsum",
    )(f_logit, bias, tri)


def _router_kernel(x_ref, g_ref, whi_ref, wlo_ref, o_ref):
    h = _rms(x_ref[...], g_ref[...])
    h_hi = h.astype(BF16)
    h_lo = (h - h_hi.astype(F32)).astype(BF16)
    logits = _dot(h_hi, whi_ref[...]) + _dot(h_lo, whi_ref[...]) + _dot(h_hi, wlo_ref[...])
    lane = lax.broadcasted_iota(jnp.int32, logits.shape, 1)
    logits = jnp.where(lane < N_EXPERTS, logits, -jnp.inf)
    m1 = jnp.max(logits, axis=1, keepdims=True)
    i1 = jnp.min(jnp.where(logits == m1, lane, LANES), axis=1, keepdims=True)
    rest = jnp.where(lane == i1, -jnp.inf, logits)
    m2 = jnp.max(rest, axis=1, keepdims=True)
    i2 = jnp.min(jnp.where(rest == m2, lane, LANES), axis=1, keepdims=True)
    e2 = jnp.exp(m2 - m1)
    g1 = 1.0 / (1.0 + e2)
    out = jnp.where(lane == 0, g1, 0.0) + jnp.where(lane == 1, e2 * g1, 0.0)
    out = out + jnp.where(lane == 2, i1.astype(F32), 0.0) + jnp.where(lane == 3, i2.astype(F32), 0.0)
    o_ref[...] = out


def _router(x, gain, w_router, tm=1024):
    t, d = x.shape
    w = jnp.zeros((d, LANES), F32).at[:, :N_EXPERTS].set(w_router)
    w_hi = w.astype(BF16)
    w_lo = (w - w_hi.astype(F32)).astype(BF16)
    return pl.pallas_call(
        _router_kernel, grid=(t // tm,),
        in_specs=[pl.BlockSpec((tm, d), lambda i: (i, 0)), pl.BlockSpec((1, d), lambda i: (0, 0)),
                  pl.BlockSpec((d, LANES), lambda i: (0, 0)), pl.BlockSpec((d, LANES), lambda i: (0, 0))],
        out_specs=pl.BlockSpec((tm, LANES), lambda i: (i, 0)),
        out_shape=jax.ShapeDtypeStruct((t, LANES), F32),
        compiler_params=_params("parallel"), name="router",
    )(x, gain.reshape(1, d), w_hi, w_lo)


def _route_plan(route, tm):
    t = route.shape[0]
    n = TOP_K * t
    n_tiles = n // tm
    expert = route[:, 2:2 + TOP_K].astype(jnp.int32).T.reshape(-1)
    onehot = (expert[:, None] == jnp.arange(N_EXPERTS, dtype=jnp.int32)[None, :]).astype(jnp.int32)
    rank = jnp.sum((jnp.cumsum(onehot, axis=0) - onehot) * onehot, axis=1)
    counts = jnp.sum(onehot, axis=0)
    ends = jnp.cumsum(counts)
    pos = (ends - counts)[expert] + rank
    tile_start = jnp.arange(n_tiles, dtype=jnp.int32) * tm
    lo = jnp.sort(jnp.concatenate([tile_start, ends[:-1].astype(jnp.int32)]))
    hi = jnp.concatenate([lo[1:], jnp.full((1,), n, jnp.int32)])
    tile = jnp.minimum(lo // tm, n_tiles - 1)
    exp_id = jnp.minimum(jnp.sum((lo[:, None] >= ends[None, :]).astype(jnp.int32), axis=1), N_EXPERTS - 1)
    first = jnp.concatenate([jnp.ones((1,), jnp.int32), (tile[1:] != tile[:-1]).astype(jnp.int32)])
    last = jnp.concatenate([(tile[1:] != tile[:-1]).astype(jnp.int32), jnp.ones((1,), jnp.int32)])
    visits = jnp.stack([tile, exp_id, lo, hi, first + 2 * last]).astype(jnp.int32)
    return pos.reshape(TOP_K, t).astype(jnp.int32), visits


def _slot_indices(pos, rows):
    t = pos.shape[1]
    return pos.reshape(TOP_K, t // rows, rows).transpose(1, 0, 2).reshape(t // rows, TOP_K * rows)


def _dispatch_kernel(rows, idx_hbm, x_ref, xs_hbm, idx_smem, idx_sem, row_sem):
    idx_copy = pltpu.make_async_copy(idx_hbm.at[pl.program_id(0)], idx_smem, idx_sem)
    idx_copy.start()
    idx_copy.wait()

    def issue(r, _):
        for k in range(TOP_K):
            pltpu.make_async_copy(x_ref.at[pl.ds(r, 1)], xs_hbm.at[pl.ds(idx_smem[k * rows + r], 1)], row_sem).start()
        return 0

    lax.fori_loop(0, rows, issue, 0, unroll=8)
    done = xs_hbm.at[pl.ds(0, TOP_K * rows)]
    pltpu.make_async_copy(done, done, row_sem).wait()


def _dispatch(x, pos, rows=DISPATCH_ROWS):
    t, d = x.shape
    return pl.pallas_call(
        functools.partial(_dispatch_kernel, rows), grid=(t // rows,),
        in_specs=[pl.BlockSpec(memory_space=pl.ANY), pl.BlockSpec((rows, d), lambda i: (i, 0))],
        out_specs=pl.BlockSpec(memory_space=pl.ANY),
        out_shape=jax.ShapeDtypeStruct((TOP_K * t, d), x.dtype),
        scratch_shapes=[pltpu.SMEM((TOP_K * rows,), jnp.int32), pltpu.SemaphoreType.DMA(()),
                        pltpu.SemaphoreType.DMA(())],
        compiler_params=_params("arbitrary"), name="moe_dispatch",
    )(_slot_indices(pos, rows), x)


def _experts_kernel(v_ref, x_ref, g_ref, wg_ref, wu_ref, wd_ref, o_ref, h_ref, acc_ref):
    v, f = pl.program_id(0), pl.program_id(1)
    tm = x_ref.shape[0]
    lo, hi, flags = v_ref[2, v], v_ref[3, v], v_ref[4, v]

    @pl.when(jnp.logical_and((flags & 1) == 1, f == 0))
    def _():
        h_ref[...] = _rms(x_ref[...], g_ref[...]).astype(BF16)
        acc_ref[...] = jnp.zeros_like(acc_ref)

    @pl.when(hi > lo)
    def _():
        h = h_ref[...]
        gate = _dot(h, wg_ref[0])
        up = _dot(h, wu_ref[0])
        row = lax.broadcasted_iota(jnp.int32, (tm, 1), 0) + v_ref[0, v] * tm
        mine = jnp.logical_and(row >= lo, row < hi)
        act = jnp.where(mine, gate * jax.nn.sigmoid(gate) * up, 0.0)
        acc_ref[...] += _dot(act.astype(BF16), wd_ref[0])

    @pl.when(jnp.logical_and((flags & 2) == 2, f == pl.num_programs(1) - 1))
    def _():
        o_ref[...] = acc_ref[...]


def _experts(xs, gain, visits, w_gate, w_up, w_down, tm, tf=512):
    n, d = xs.shape
    ff = w_gate.shape[2]

    def row(v, f, vis):
        return vis[0, v], 0

    grid_spec = pltpu.PrefetchScalarGridSpec(
        num_scalar_prefetch=1, grid=(visits.shape[1], ff // tf),
        in_specs=[pl.BlockSpec((tm, d), row), pl.BlockSpec((1, d), lambda v, f, vis: (0, 0)),
                  pl.BlockSpec((1, d, tf), lambda v, f, vis: (vis[1, v], 0, f)),
                  pl.BlockSpec((1, d, tf), lambda v, f, vis: (vis[1, v], 0, f)),
                  pl.BlockSpec((1, tf, d), lambda v, f, vis: (vis[1, v], f, 0))],
        out_specs=pl.BlockSpec((tm, d), row),
        scratch_shapes=[pltpu.VMEM((tm, d), BF16), pltpu.VMEM((tm, d), F32)])
    return pl.pallas_call(
        _experts_kernel, grid_spec=grid_spec, out_shape=jax.ShapeDtypeStruct((n, d), F32),
        compiler_params=_params("arbitrary", "arbitrary"), name="experts",
    )(visits, xs, gain.reshape(1, d), w_gate, w_up, w_down)


def _combine_kernel(rows, idx_hbm, x_ref, route_ref, g_ref, y_hbm, o_ref, idx_smem, buf, idx_sem, row_sem):
    idx_copy = pltpu.make_async_copy(idx_hbm.at[pl.program_id(0)], idx_smem, idx_sem)
    idx_copy.start()
    idx_copy.wait()

    def issue(r, _):
        for k in range(TOP_K):
            pltpu.make_async_copy(y_hbm.at[pl.ds(idx_smem[k * rows + r], 1)], buf.at[k, pl.ds(r, 1)], row_sem).start()
        return 0

    lax.fori_loop(0, rows, issue, 0, unroll=8)
    pltpu.make_async_copy(buf, buf, row_sem).wait()
    out = x_ref[...]
    for k in range(TOP_K):
        out = out + route_ref[:, k:k + 1] * buf[k]
    o_ref[...] = _rms(out, g_ref[...])


def _combine(x, y, pos, route, gain, rows=COMBINE_ROWS):
    t, d = x.shape
    return pl.pallas_call(
        functools.partial(_combine_kernel, rows), grid=(t // rows,),
        in_specs=[pl.BlockSpec(memory_space=pl.ANY), pl.BlockSpec((rows, d), lambda i: (i, 0)),
                  pl.BlockSpec((rows, LANES), lambda i: (i, 0)), pl.BlockSpec((1, d), lambda i: (0, 0)),
                  pl.BlockSpec(memory_space=pl.ANY)],
        out_specs=pl.BlockSpec((rows, d), lambda i: (i, 0)),
        out_shape=jax.ShapeDtypeStruct((t, d), F32),
        scratch_shapes=[pltpu.SMEM((TOP_K * rows,), jnp.int32), pltpu.VMEM((TOP_K, rows, d), F32),
                        pltpu.SemaphoreType.DMA(()), pltpu.SemaphoreType.DMA(())],
        compiler_params=_params("arbitrary"), name="moe_combine",
    )(_slot_indices(pos, rows), x, route, gain.reshape(1, d), y)


def _swiglu_kernel(x_ref, g_ref, wg_ref, wu_ref, wd_ref, o_ref, h_ref, acc_ref):
    f = pl.program_id(1)

    @pl.when(f == 0)
    def _():
        h_ref[...] = _rms(x_ref[...], g_ref[...]).astype(BF16)
        acc_ref[...] = jnp.zeros_like(acc_ref)

    h = h_ref[...]
    gate = _dot(h, wg_ref[...])
    up = _dot(h, wu_ref[...])
    act = gate * jax.nn.sigmoid(gate) * up
    acc_ref[...] += _dot(act.astype(BF16), wd_ref[...])

    @pl.when(f == pl.num_programs(1) - 1)
    def _():
        o_ref[...] = x_ref[...] + acc_ref[...]


def _swiglu(x, gain, w_gate, w_up, w_down, tm=1024, tf=256):
    t, d = x.shape
    ff = w_gate.shape[1]
    row = pl.BlockSpec((tm, d), lambda i, f: (i, 0))
    return pl.pallas_call(
        _swiglu_kernel, grid=(t // tm, ff // tf),
        in_specs=[row, pl.BlockSpec((1, d), lambda i, f: (0, 0)),
                  pl.BlockSpec((d, tf), lambda i, f: (0, f)), pl.BlockSpec((d, tf), lambda i, f: (0, f)),
                  pl.BlockSpec((tf, d), lambda i, f: (f, 0))],
        out_specs=row, out_shape=jax.ShapeDtypeStruct((t, d), F32),
        scratch_shapes=[pltpu.VMEM((tm, d), BF16), pltpu.VMEM((tm, d), F32)],
        compiler_params=_params("parallel", "arbitrary"), name="swiglu",
    )(x, gain.reshape(1, d), w_gate, w_up, w_down)


def _even_layer(x, batch, seqlen, norm_mix, w_in, s5, w_glu, b_glu, d_skip, w_out, norm_ffn, w_gate, w_up, w_down):
    s5_width = d_skip.size
    sb_width = (w_in.shape[1] - s5_width) // 3
    scale = LOG2_E / math.sqrt(SB_HEAD_DIM)
    w_u = w_in[:, :s5_width].astype(BF16)
    w_q = (w_in[:, s5_width:s5_width + sb_width] * scale).astype(BF16)
    w_kv = w_in[:, s5_width + sb_width:].astype(BF16)
    u, qt, kv = _norm_proj(x, norm_mix, [w_u, w_q.T, w_kv], [F32, BF16, BF16], [False, True, False], batch)

    n_scan = int(math.log2(seqlen // S5_CHUNK))
    tables = _s5_tables(*s5, n_scan)
    y = _s5_conv(u, tables, batch, seqlen)
    a_out = _s5_glu(y, u, d_skip, w_glu, b_glu)

    b_out_t = _sb_attention(qt, kv.reshape(batch, seqlen, 2 * sb_width))

    w_out = w_out.astype(BF16)
    x = _proj_residual(x, [a_out, b_out_t], [w_out[:s5_width], w_out[s5_width:]], [False, True], batch)
    return _swiglu(x, norm_ffn, w_gate.astype(BF16), w_up.astype(BF16), w_down.astype(BF16))


def _odd_layer(x, batch, seqlen, norm_mix, w_in, b_f, w_out, norm_ffn, w_router, w_gate, w_up, w_down, final_gain):
    heads = b_f.size
    width = heads * FOX_HEAD_DIM
    scale = LOG2_E / math.sqrt(FOX_HEAD_DIM)
    d = x.shape[1]
    w_q = (w_in[:, :width] * scale).astype(BF16)
    w_kv = w_in[:, width:3 * width].astype(BF16)
    w_f = jnp.zeros((d, LANES), F32).at[:, :heads].set(w_in[:, 3 * width:]).astype(BF16)
    qt, kv, f_logit = _norm_proj(x, norm_mix, [w_q.T, w_kv, w_f], [BF16, BF16, F32], [True, False, False], batch)
    bias = jnp.zeros((1, LANES), F32).at[0, :heads].set(b_f)
    cum = _fox_gate_cumsum(f_logit, bias, batch, seqlen)

    c_out_t = _fox_attention(qt, kv.reshape(batch, seqlen, 2 * width), cum.reshape(batch, seqlen, LANES))
    x = _proj_residual(x, [c_out_t], [w_out.astype(BF16)], [True], batch)
    route = _router(x, norm_ffn, w_router)
    pos, visits = _route_plan(route, MOE_TILE)
    xs = _dispatch(x, pos)
    ys = _experts(xs, norm_ffn, visits, w_gate.astype(BF16), w_up.astype(BF16), w_down.astype(BF16), MOE_TILE)
    return _combine(x, ys, pos, route, final_gain)


def kernel(x, ev_norm_mix, ev_w_in, s5_a_re, s5_a_im, s5_b_re, s5_b_im, s5_c_re, s5_c_im, s5_d, s5_log_step,
           s5_w_glu, s5_b_glu, ev_w_out, ev_norm_ffn, ffn_w_gate, ffn_w_up, ffn_w_down, od_norm_mix, od_w_in,
           fox_b_f, od_w_out, od_norm_ffn, moe_w_router, moe_w_gate, moe_w_up, moe_w_down, final_norm):
    batch, seqlen, d = x.shape
    assert ev_w_in.shape[0] == 1 and od_w_in.shape[0] == 1, "one even and one odd layer"
    h = x.reshape(batch * seqlen, d)
    s5 = (s5_a_re[0], s5_a_im[0], s5_b_re[0], s5_b_im[0], s5_c_re[0], s5_c_im[0], s5_log_step[0])
    h = _even_layer(h, batch, seqlen, ev_norm_mix[0], ev_w_in[0], s5, s5_w_glu[0], s5_b_glu[0], s5_d[0].reshape(-1),
                    ev_w_out[0], ev_norm_ffn[0], ffn_w_gate[0], ffn_w_up[0], ffn_w_down[0])
    h = _odd_layer(h, batch, seqlen, od_norm_mix[0], od_w_in[0], fox_b_f[0], od_w_out[0], od_norm_ffn[0],
                   moe_w_router[0], moe_w_gate[0], moe_w_up[0], moe_w_down[0], final_norm)
    return h.reshape(batch, seqlen, d)
```

```python
import functools
import math

import jax
import jax.numpy as jnp
from jax import lax
from jax.experimental import pallas as pl
from jax.experimental.pallas import tpu as pltpu

F32 = jnp.float32
BF16 = jnp.bfloat16

RMS_EPS = 1e-6
LANES = 128
VMEM_LIMIT_BYTES = 56 * 1024 * 1024

S5_GROUP = 16
S5_STATE = 64
S5_CHUNK = 8
SB_HEAD_DIM = 64
FOX_HEAD_DIM = 64
N_EXPERTS = 8
TOP_K = 2
MOE_TILE = 1024
DISPATCH_ROWS = 512
COMBINE_ROWS = 512
MXU_TILE = 256
ATT_BQ = 1024
ATT_BK = 1024
LOG2_E = math.log2(math.e)


def _params(*sem):
    return pltpu.CompilerParams(dimension_semantics=sem, vmem_limit_bytes=VMEM_LIMIT_BYTES)


def _rms(x, g):
    ms = jnp.mean(x * x, axis=-1, keepdims=True)
    return x * lax.rsqrt(ms + RMS_EPS) * g


def _dot(a, b):
    return jnp.dot(a, b, preferred_element_type=F32)


_CONTRACT_LAST = (((1,), (1,)), ((), ()))
_CONTRACT_FIRST = (((0,), (0,)), ((), ()))


def _norm_proj_kernel(feature_major, x_ref, g_ref, *refs):
    n_out = len(feature_major)
    w_refs, o_refs = refs[:n_out], refs[n_out:]
    h = _rms(x_ref[...], g_ref[...]).astype(BF16)
    for fm, w_ref, o_ref in zip(feature_major, w_refs, o_refs):
        if fm:
            o_ref[0] = lax.dot_general(w_ref[...], h, _CONTRACT_LAST, preferred_element_type=F32).astype(o_ref.dtype)
        else:
            o_ref[...] = _dot(h, w_ref[...]).astype(o_ref.dtype)


def _norm_proj(x, gain, weights, out_dtypes, feature_major, batch, tm=512):
    t, d = x.shape
    per_seq = t // batch // tm
    in_specs = [pl.BlockSpec((tm, d), lambda i: (i, 0)), pl.BlockSpec((1, d), lambda i: (0, 0))]
    in_specs += [pl.BlockSpec(w.shape, lambda i: (0, 0)) for w in weights]
    out_specs, out_shape = [], []
    for w, dt, fm in zip(weights, out_dtypes, feature_major):
        if fm:
            out_specs.append(pl.BlockSpec((1, w.shape[0], tm), lambda i: (i // per_seq, 0, i % per_seq)))
            out_shape.append(jax.ShapeDtypeStruct((batch, w.shape[0], t // batch), dt))
        else:
            out_specs.append(pl.BlockSpec((tm, w.shape[1]), lambda i: (i, 0)))
            out_shape.append(jax.ShapeDtypeStruct((t, w.shape[1]), dt))
    return pl.pallas_call(
        functools.partial(_norm_proj_kernel, tuple(feature_major)),
        grid=(t // tm,), in_specs=in_specs, out_specs=out_specs, out_shape=out_shape,
        compiler_params=_params("parallel"), name="norm_proj",
    )(x, gain.reshape(1, d), *weights)


def _proj_residual_kernel(feature_major, res_ref, *refs):
    n_in = len(feature_major)
    a_refs, w_refs, o_ref = refs[:n_in], refs[n_in:2 * n_in], refs[2 * n_in]
    acc = res_ref[...]
    for fm, a_ref, w_ref in zip(feature_major, a_refs, w_refs):
        if fm:
            acc = acc + lax.dot_general(a_ref[0], w_ref[...], _CONTRACT_FIRST, preferred_element_type=F32)
        else:
            acc = acc + _dot(a_ref[...], w_ref[...])
    o_ref[...] = acc


def _proj_residual(res, acts, weights, feature_major, batch, tm=512):
    t, n = res.shape
    per_seq = t // batch // tm
    in_specs = [pl.BlockSpec((tm, n), lambda i: (i, 0))]
    for a, fm in zip(acts, feature_major):
        if fm:
            in_specs.append(pl.BlockSpec((1, a.shape[1], tm), lambda i: (i // per_seq, 0, i % per_seq)))
        else:
            in_specs.append(pl.BlockSpec((tm, a.shape[1]), lambda i: (i, 0)))
    in_specs += [pl.BlockSpec(w.shape, lambda i: (0, 0)) for w in weights]
    return pl.pallas_call(
        functools.partial(_proj_residual_kernel, tuple(feature_major)),
        grid=(t // tm,), in_specs=in_specs, out_specs=pl.BlockSpec((tm, n), lambda i: (i, 0)),
        out_shape=jax.ShapeDtypeStruct((t, n), F32),
        compiler_params=_params("parallel"), name="proj_residual",
    )(res, *acts, *weights)


def _spread_selector(groups, outer, inner):
    dst = jnp.arange(outer * groups * inner)
    src = (dst // (groups * inner)) * inner + dst % inner
    grp = (dst // inner) % groups
    hit = (grp[None, None, :] == jnp.arange(groups)[:, None, None]) & (src[None, None, :] == jnp.arange(outer * inner)[None, :, None])
    return hit.astype(F32)


def _spread(tab, row_sel, col_sel, nb):
    g = tab.shape[0]
    tab = tab.astype(BF16).astype(F32).reshape(nb, g // nb, tab.shape[1], tab.shape[2])
    cols = jnp.einsum('xgab,gbd->xgad', tab, col_sel)
    return jnp.einsum('gas,xgad->xsd', row_sel, cols).astype(BF16)


def _s5_tables(a_re, a_im, b_re, b_im, c_re, c_im, log_step, n_scan):
    g, p = a_re.shape
    h = b_re.shape[-1]
    ch = S5_CHUNK
    gl = LANES // h
    nb = g // gl
    lam = lax.complex(a_re, a_im)
    dl = lam * jnp.exp(log_step)[:, None]
    lam_bar = jnp.exp(dl)
    b_bar = ((lam_bar - 1.0) / lam)[..., None] * lax.complex(b_re, b_im)
    c = lax.complex(c_re, c_im)
    steps = jnp.arange(ch + 1, dtype=a_re.dtype)
    pw = jnp.exp(dl[:, None, :] * steps[None, :, None])
    kd = jnp.einsum('gop,gdp,gpi->gdoi', c, pw[:, :ch], b_bar).real
    lag = jnp.arange(ch)[None, :] - jnp.arange(ch)[:, None]
    kfull = kd[:, jnp.clip(lag, 0, ch - 1)]
    kfull = jnp.where((lag >= 0)[None, :, :, None, None], kfull, 0.0)
    k_grp = kfull.transpose(0, 1, 4, 2, 3).reshape(g, ch * h, ch * h)
    bc = pw[:, ch - 1 - jnp.arange(ch)][:, :, :, None] * b_bar[:, None]
    bc = bc.transpose(0, 1, 3, 2).reshape(g, ch * h, p)
    b_grp = jnp.concatenate([bc.real, bc.imag], axis=-1)
    m = c[:, None] * pw[:, 1:ch + 1][:, :, None, :]
    m = m.transpose(0, 3, 1, 2).reshape(g, p, ch * h)
    c_grp = jnp.concatenate([m.real, -m.imag], axis=1)
    chunk_sel = _spread_selector(gl, ch, h)
    state_sel = _spread_selector(gl, 2, p)
    k_tab = _spread(k_grp, chunk_sel, chunk_sel, nb)
    b_tab = _spread(b_grp, chunk_sel, state_sel, nb)
    c_tab = _spread(c_grp, state_sel, chunk_sel, nb)
    pows = (ch * (2 ** jnp.arange(n_scan))).astype(a_re.dtype)
    ak = jnp.exp(dl[:, None, :] * pows[None, :, None])
    ak = ak.reshape(nb, gl, n_scan, p).transpose(0, 2, 1, 3).reshape(nb, n_scan, gl * p)
    p1 = jnp.concatenate([ak.real, ak.real], axis=-1)
    p2 = jnp.concatenate([-ak.imag, ak.imag], axis=-1)
    return k_tab, b_tab, c_tab, p1, p2


def _s5_kernel(n_scan, u_ref, k_ref, b_ref, c_ref, p1_ref, p2_ref, y_ref):
    ch = S5_CHUNK
    rows = u_ref.shape[0] // ch
    u = jnp.concatenate([u_ref[pl.ds(t, rows, stride=ch), :].astype(BF16) for t in range(ch)], axis=1)
    state = _dot(u, b_ref[0])
    row = lax.broadcasted_iota(jnp.int32, state.shape, 0)
    half = state.shape[1] // 2
    for k in range(n_scan):
        shift = 1 << k
        prev = jnp.where(row >= shift, pltpu.roll(state, shift, axis=0), 0.0)
        state = state + prev * p1_ref[0, k:k + 1, :] + pltpu.roll(prev, half, axis=1) * p2_ref[0, k:k + 1, :]
    carried = jnp.where(row >= 1, pltpu.roll(state, 1, axis=0), 0.0)
    y = _dot(u, k_ref[0]) + _dot(carried.astype(BF16), c_ref[0])
    for t in range(ch):
        y_ref[pl.ds(t, rows, stride=ch), :] = y[:, t * LANES:(t + 1) * LANES]


def _s5_conv(u, tables, batch, seqlen):
    k_tab, b_tab, c_tab, p1, p2 = tables
    t, width = u.shape
    nb = width // LANES
    n_scan = p1.shape[1]
    kw, sw = k_tab.shape[1], b_tab.shape[2]
    blk = pl.BlockSpec((seqlen, LANES), lambda x, b: (b, x))
    return pl.pallas_call(
        functools.partial(_s5_kernel, n_scan), grid=(nb, batch),
        in_specs=[blk,
                  pl.BlockSpec((1, kw, kw), lambda x, b: (x, 0, 0)),
                  pl.BlockSpec((1, kw, sw), lambda x, b: (x, 0, 0)),
                  pl.BlockSpec((1, sw, kw), lambda x, b: (x, 0, 0)),
                  pl.BlockSpec((1, n_scan, sw), lambda x, b: (x, 0, 0)),
                  pl.BlockSpec((1, n_scan, sw), lambda x, b: (x, 0, 0))],
        out_specs=blk, out_shape=jax.ShapeDtypeStruct((t, width), F32),
        compiler_params=_params("parallel", "parallel"), name="s5_conv",
    )(u, k_tab, b_tab, c_tab, p1, p2)


def _gelu_tanh(x):
    return 0.5 * x * (1.0 + jnp.tanh(math.sqrt(2.0 / math.pi) * (x + 0.044715 * (x * x * x))))


def _s5_glu_kernel(y_ref, u_ref, d_ref, w_ref, b_ref, o_ref):
    y = _gelu_tanh(y_ref[...] + d_ref[...] * u_ref[...])
    gate = jax.nn.sigmoid(_dot(y.astype(BF16), w_ref[...]) + b_ref[...])
    o_ref[...] = (y * gate).astype(o_ref.dtype)


def _s5_glu(y, u, d_skip, w_glu, b_glu, tm=1024):
    t, width = y.shape
    row = pl.BlockSpec((tm, width), lambda i: (i, 0))
    vec = pl.BlockSpec((1, width), lambda i: (0, 0))
    return pl.pallas_call(
        _s5_glu_kernel, grid=(t // tm,),
        in_specs=[row, row, vec, pl.BlockSpec((width, width), lambda i: (0, 0)), vec],
        out_specs=row, out_shape=jax.ShapeDtypeStruct((t, width), BF16),
        compiler_params=_params("parallel"), name="s5_glu",
    )(y, u, d_skip.reshape(1, width), w_glu.astype(BF16), b_glu.reshape(1, width))


def _diag_valid(bk, bq, d, strict):
    key = lax.broadcasted_iota(jnp.int32, (bk, bq), 0) + d * bk
    qry = lax.broadcasted_iota(jnp.int32, (bk, bq), 1)
    return key < qry if strict else key <= qry


def _softplus2(z):
    return jnp.maximum(z, 0.0) + jnp.log2(1.0 + jnp.exp2(-jnp.abs(z)))


def _key_rows(j, bk):
    return pl.ds(pl.multiple_of(j * bk, bk), bk)


def _sb_kernel(bk, qt_ref, k_ref, v_ref, tri_ref, ot_ref):
    qi = pl.program_id(2)
    width, bq = qt_ref.shape[1], qt_ref.shape[2]
    dh = width // 2
    tb = tri_ref.shape[0]
    ratio = bq // bk
    feat = lax.broadcasted_iota(jnp.int32, (width, bq), 0)
    qt2 = qt_ref[0]

    def head(p):
        qt = jnp.where((feat >= p * dh) & (feat < (p + 1) * dh), qt2, jnp.zeros_like(qt2))

        def block(j, carry, acc, diag):
            rows = _key_rows(j, bk)
            z = _dot(k_ref[0, rows, :], qt)
            sp = _softplus2(z)
            if diag is not None:
                valid = _diag_valid(bk, bq, diag, True)
                sp = jnp.where(valid, sp, 0.0)
            sp = sp.astype(BF16)
            parts = []
            for t in reversed(range(bk // tb)):
                sub = slice(t * tb, (t + 1) * tb)
                part = _dot(tri_ref[...], sp[sub]) + carry
                carry = part[0:1, :]
                parts.insert(0, part)
            w = jnp.exp2(z - jnp.concatenate(parts, axis=0))
            if diag is not None:
                w = jnp.where(valid, w, 0.0)
            pv = lax.dot_general(v_ref[0, rows, :], w.astype(BF16), _CONTRACT_FIRST, preferred_element_type=F32)
            return carry, acc + pv

        carry = jnp.zeros((1, bq), F32)
        acc = jnp.zeros((width, bq), F32)
        for d in reversed(range(ratio)):
            carry, acc = block(qi * ratio + d, carry, acc, d)
        carry, acc = lax.fori_loop(0, qi * ratio, lambda i, c: block(qi * ratio - 1 - i, c[0], c[1], None),
                                   (carry, acc))
        return acc

    acc0, acc1 = head(0), head(1)
    ot_ref[0] = jnp.where(feat < dh, acc0, acc1).astype(ot_ref.dtype)


def _sb_attention(qt, kv, bq=ATT_BQ, bk=ATT_BK):
    batch, width, seqlen = qt.shape
    pairs = width // LANES
    tb = MXU_TILE
    tri = (lax.broadcasted_iota(jnp.int32, (tb, tb), 0) <= lax.broadcasted_iota(jnp.int32, (tb, tb), 1)).astype(BF16)
    qspec = pl.BlockSpec((1, LANES, bq), lambda b, h, q: (b, h, q))
    return pl.pallas_call(
        functools.partial(_sb_kernel, bk), grid=(batch, pairs, seqlen // bq),
        in_specs=[qspec,
                  pl.BlockSpec((1, seqlen, LANES), lambda b, h, q: (b, 0, h)),
                  pl.BlockSpec((1, seqlen, LANES), lambda b, h, q: (b, 0, pairs + h)),
                  pl.BlockSpec((tb, tb), lambda b, h, q: (0, 0))],
        out_specs=qspec, out_shape=jax.ShapeDtypeStruct(qt.shape, BF16),
        compiler_params=_params("parallel", "parallel", "arbitrary"), name="sb_attention",
    )(qt, kv, kv, tri)


def _fox_kernel(bk, qt_ref, k_ref, v_ref, cum_ref, ot_ref, bias_ref):
    hp, qi = pl.program_id(1), pl.program_id(2)
    width, bq = qt_ref.shape[1], qt_ref.shape[2]
    dh = width // 2
    ratio = bq // bk
    feat = lax.broadcasted_iota(jnp.int32, (width, bq), 0)
    vlane = lax.broadcasted_iota(jnp.int32, (bk, width), 1)
    qt2 = qt_ref[0]

    @pl.when(qi == 0)
    def _():
        c = cum_ref[0]
        head_lane = lax.broadcasted_iota(jnp.int32, c.shape, 1)
        for p in range(2):
            col = jnp.sum(jnp.where(head_lane == 2 * hp + p, c, 0.0), axis=1, keepdims=True)
            bias_ref[p] = jnp.broadcast_to(-LOG2_E * col, c.shape)

    def head(p):
        mine = (feat >= p * dh) & (feat < (p + 1) * dh)
        qt = jnp.where(mine, qt2, jnp.zeros_like(qt2))
        vmine = (vlane >= p * dh) & (vlane < (p + 1) * dh)

        def block(j, m, acc, diag):
            rows = _key_rows(j, bk)
            bias = bias_ref[p, rows, :]
            s = _dot(k_ref[0, rows, :], qt) + jnp.concatenate([bias] * (bq // LANES), axis=1)
            if diag is not None:
                s = jnp.where(_diag_valid(bk, bq, diag, False), s, -jnp.inf)
            m_new = jnp.maximum(m, jnp.max(s, axis=0, keepdims=True))
            prob = jnp.exp2(s - m_new).astype(BF16)
            v = v_ref[0, rows, :]
            v = jnp.where(vmine, v, jnp.ones_like(v))
            pv = lax.dot_general(v, prob, _CONTRACT_FIRST, preferred_element_type=F32)
            return m_new, jnp.exp2(m - m_new) * acc + pv

        m = jnp.full((1, bq), -jnp.inf, F32)
        acc = jnp.zeros((width, bq), F32)
        m, acc = lax.fori_loop(0, qi * ratio, lambda j, c: block(j, c[0], c[1], None), (m, acc))
        for d in range(ratio):
            m, acc = block(qi * ratio + d, m, acc, d)
        other = (1 - p) * dh
        return acc / acc[other:other + 1]

    out0, out1 = head(0), head(1)
    ot_ref[0] = jnp.where(feat < dh, out0, out1).astype(ot_ref.dtype)


def _fox_attention(qt, kv, cum, bq=ATT_BQ, bk=ATT_BK):
    batch, width, seqlen = qt.shape
    pairs = width // LANES
    qspec = pl.BlockSpec((1, LANES, bq), lambda b, h, q: (b, h, q))
    return pl.pallas_call(
        functools.partial(_fox_kernel, bk), grid=(batch, pairs, seqlen // bq),
        in_specs=[qspec,
                  pl.BlockSpec((1, seqlen, LANES), lambda b, h, q: (b, 0, h)),
                  pl.BlockSpec((1, seqlen, LANES), lambda b, h, q: (b, 0, pairs + h)),
                  pl.BlockSpec((1, seqlen, LANES), lambda b, h, q: (b, 0, 0))],
        out_specs=qspec, out_shape=jax.ShapeDtypeStruct(qt.shape, BF16),
        scratch_shapes=[pltpu.VMEM((2, seqlen, LANES), F32)],
        compiler_params=_params("parallel", "parallel", "arbitrary"), name="fox_attention",
    )(qt, kv, kv, cum)


def _split3(x):
    hi = x.astype(BF16)
    r = x - hi.astype(F32)
    mid = r.astype(BF16)
    lo = (r - mid.astype(F32)).astype(BF16)
    return hi, mid, lo


def _fox_gate_kernel(f_ref, b_ref, tri_ref, c_ref, carry_ref):
    @pl.when(pl.program_id(1) == 0)
    def _():
        carry_ref[...] = jnp.zeros_like(carry_ref)

    x = f_ref[...] + b_ref[...]
    log_f = jnp.minimum(x, 0.0) - jnp.log(1.0 + jnp.exp(-jnp.abs(x)))
    hi, mid, lo = _split3(log_f)
    tri = tri_ref[...]
    c = _dot(tri, hi) + _dot(tri, mid) + _dot(tri, lo) + carry_ref[...]
    c_ref[...] = c
    carry_ref[...] = c[c.shape[0] - 1:, :]


def _fox_gate_cumsum(f_logit, bias, batch, seqlen, tc=512):
    t, w = f_logit.shape
    nchunk = seqlen // tc
    tri = (lax.broadcasted_iota(jnp.int32, (tc, tc), 1) <= lax.broadcasted_iota(jnp.int32, (tc, tc), 0)).astype(BF16)
    return pl.pallas_call(
        _fox_gate_kernel, grid=(batch, nchunk),
        in_specs=[pl.BlockSpec((tc, w), lambda b, c: (b * nchunk + c, 0)),
                  pl.BlockSpec((1, w), lambda b, c: (0, 0)),
                  pl.BlockSpec((tc, tc), lambda b, c: (0, 0))],
        out_specs=pl.BlockSpec((tc, w), lambda b, c: (b * nchunk + c, 0)),
        out_shape=jax.ShapeDtypeStruct((t, w), F32),
        scratch_shapes=[pltpu.VMEM((1, w), F32)],
        compiler_params=_params("parallel", "arbitrary"), name="fox_gate_cumsum",
    )(f_logit, bias, tri)


def _router_kernel(x_ref, g_ref, whi_ref, wlo_ref, o_ref):
    h = _rms(x_ref[...], g_ref[...])
    h_hi = h.astype(BF16)
    h_lo = (h - h_hi.astype(F32)).astype(BF16)
    logits = _dot(h_hi, whi_ref[...]) + _dot(h_lo, whi_ref[...]) + _dot(h_hi, wlo_ref[...])
    lane = lax.broadcasted_iota(jnp.int32, logits.shape, 1)
    logits = jnp.where(lane < N_EXPERTS, logits, -jnp.inf)
    m1 = jnp.max(logits, axis=1, keepdims=True)
    i1 = jnp.min(jnp.where(logits == m1, lane, LANES), axis=1, keepdims=True)
    rest = jnp.where(lane == i1, -jnp.inf, logits)
    m2 = jnp.max(rest, axis=1, keepdims=True)
    i2 = jnp.min(jnp.where(rest == m2, lane, LANES), axis=1, keepdims=True)
    e2 = jnp.exp(m2 - m1)
    g1 = 1.0 / (1.0 + e2)
    out = jnp.where(lane == 0, g1, 0.0) + jnp.where(lane == 1, e2 * g1, 0.0)
    out = out + jnp.where(lane == 2, i1.astype(F32), 0.0) + jnp.where(lane == 3, i2.astype(F32), 0.0)
    o_ref[...] = out


def _router(x, gain, w_router, tm=1024):
    t, d = x.shape
    w = jnp.zeros((d, LANES), F32).at[:, :N_EXPERTS].set(w_router)
    w_hi = w.astype(BF16)
    w_lo = (w - w_hi.astype(F32)).astype(BF16)
    return pl.pallas_call(
        _router_kernel, grid=(t // tm,),
        in_specs=[pl.BlockSpec((tm, d), lambda i: (i, 0)), pl.BlockSpec((1, d), lambda i: (0, 0)),
                  pl.BlockSpec((d, LANES), lambda i: (0, 0)), pl.BlockSpec((d, LANES), lambda i: (0, 0))],
        out_specs=pl.BlockSpec((tm, LANES), lambda i: (i, 0)),
        out_shape=jax.ShapeDtypeStruct((t, LANES), F32),
        compiler_params=_params("parallel"), name="router",
    )(x, gain.reshape(1, d), w_hi, w_lo)


def _route_plan(route, tm):
    t = route.shape[0]
    n = TOP_K * t
    n_tiles = n // tm
    expert = route[:, 2:2 + TOP_K].astype(jnp.int32).T.reshape(-1)
    onehot = (expert[:, None] == jnp.arange(N_EXPERTS, dtype=jnp.int32)[None, :]).astype(jnp.int32)
    rank = jnp.sum((jnp.cumsum(onehot, axis=0) - onehot) * onehot, axis=1)
    counts = jnp.sum(onehot, axis=0)
    ends = jnp.cumsum(counts)
    pos = (ends - counts)[expert] + rank
    tile_start = jnp.arange(n_tiles, dtype=jnp.int32) * tm
    lo = jnp.sort(jnp.concatenate([tile_start, ends[:-1].astype(jnp.int32)]))
    hi = jnp.concatenate([lo[1:], jnp.full((1,), n, jnp.int32)])
    tile = jnp.minimum(lo // tm, n_tiles - 1)
    exp_id = jnp.minimum(jnp.sum((lo[:, None] >= ends[None, :]).astype(jnp.int32), axis=1), N_EXPERTS - 1)
    first = jnp.concatenate([jnp.ones((1,), jnp.int32), (tile[1:] != tile[:-1]).astype(jnp.int32)])
    last = jnp.concatenate([(tile[1:] != tile[:-1]).astype(jnp.int32), jnp.ones((1,), jnp.int32)])
    visits = jnp.stack([tile, exp_id, lo, hi, first + 2 * last]).astype(jnp.int32)
    return pos.reshape(TOP_K, t).astype(jnp.int32), visits


def _slot_indices(pos, rows):
    t = pos.shape[1]
    return pos.reshape(TOP_K, t // rows, rows).transpose(1, 0, 2).reshape(t // rows, TOP_K * rows)


def _dispatch_kernel(rows, idx_hbm, x_ref, xs_hbm, idx_smem, idx_sem, row_sem):
    idx_copy = pltpu.make_async_copy(idx_hbm.at[pl.program_id(0)], idx_smem, idx_sem)
    idx_copy.start()
    idx_copy.wait()

    def issue(r, _):
        for k in range(TOP_K):
            pltpu.make_async_copy(x_ref.at[pl.ds(r, 1)], xs_hbm.at[pl.ds(idx_smem[k * rows + r], 1)], row_sem).start()
        return 0

    lax.fori_loop(0, rows, issue, 0, unroll=8)
    done = xs_hbm.at[pl.ds(0, TOP_K * rows)]
    pltpu.make_async_copy(done, done, row_sem).wait()


def _dispatch(x, pos, rows=DISPATCH_ROWS):
    t, d = x.shape
    return pl.pallas_call(
        functools.partial(_dispatch_kernel, rows), grid=(t // rows,),
        in_specs=[pl.BlockSpec(memory_space=pl.ANY), pl.BlockSpec((rows, d), lambda i: (i, 0))],
        out_specs=pl.BlockSpec(memory_space=pl.ANY),
        out_shape=jax.ShapeDtypeStruct((TOP_K * t, d), x.dtype),
        scratch_shapes=[pltpu.SMEM((TOP_K * rows,), jnp.int32), pltpu.SemaphoreType.DMA(()),
                        pltpu.SemaphoreType.DMA(())],
        compiler_params=_params("arbitrary"), name="moe_dispatch",
    )(_slot_indices(pos, rows), x)


def _experts_kernel(v_ref, x_ref, g_ref, wg_ref, wu_ref, wd_ref, o_ref, h_ref, acc_ref):
    v, f = pl.program_id(0), pl.program_id(1)
    tm = x_ref.shape[0]
    lo, hi, flags = v_ref[2, v], v_ref[3, v], v_ref[4, v]

    @pl.when(jnp.logical_and((flags & 1) == 1, f == 0))
    def _():
        h_ref[...] = _rms(x_ref[...], g_ref[...]).astype(BF16)
        acc_ref[...] = jnp.zeros_like(acc_ref)

    @pl.when(hi > lo)
    def _():
        h = h_ref[...]
        gate = _dot(h, wg_ref[0].astype(BF16))
        up = _dot(h, wu_ref[0].astype(BF16))
        row = lax.broadcasted_iota(jnp.int32, (tm, 1), 0) + v_ref[0, v] * tm
        mine = jnp.logical_and(row >= lo, row < hi)
        act = jnp.where(mine, gate * jax.nn.sigmoid(gate) * up, 0.0)
        acc_ref[...] += _dot(act.astype(BF16), wd_ref[0].astype(BF16))

    @pl.when(jnp.logical_and((flags & 2) == 2, f == pl.num_programs(1) - 1))
    def _():
        o_ref[...] = acc_ref[...]


def _experts(xs, gain, visits, w_gate, w_up, w_down, tm, tf=512):
    n, d = xs.shape
    ff = w_gate.shape[2]

    def row(v, f, vis):
        return vis[0, v], 0

    grid_spec = pltpu.PrefetchScalarGridSpec(
        num_scalar_prefetch=1, grid=(visits.shape[1], ff // tf),
        in_specs=[pl.BlockSpec((tm, d), row), pl.BlockSpec((1, d), lambda v, f, vis: (0, 0)),
                  pl.BlockSpec((1, d, tf), lambda v, f, vis: (vis[1, v], 0, f)),
                  pl.BlockSpec((1, d, tf), lambda v, f, vis: (vis[1, v], 0, f)),
                  pl.BlockSpec((1, tf, d), lambda v, f, vis: (vis[1, v], f, 0))],
        out_specs=pl.BlockSpec((tm, d), row),
        scratch_shapes=[pltpu.VMEM((tm, d), BF16), pltpu.VMEM((tm, d), F32)])
    return pl.pallas_call(
        _experts_kernel, grid_spec=grid_spec, out_shape=jax.ShapeDtypeStruct((n, d), F32),
        compiler_params=_params("arbitrary", "arbitrary"), name="experts",
    )(visits, xs, gain.reshape(1, d), w_gate, w_up, w_down)


def _combine_kernel(rows, idx_hbm, x_ref, route_ref, g_ref, y_hbm, o_ref, idx_smem, buf, idx_sem, row_sem):
    idx_copy = pltpu.make_async_copy(idx_hbm.at[pl.program_id(0)], idx_smem, idx_sem)
    idx_copy.start()
    idx_copy.wait()

    def issue(r, _):
        for k in range(TOP_K):
            pltpu.make_async_copy(y_hbm.at[pl.ds(idx_smem[k * rows + r], 1)], buf.at[k, pl.ds(r, 1)], row_sem).start()
        return 0

    lax.fori_loop(0, rows, issue, 0, unroll=8)
    pltpu.make_async_copy(buf, buf, row_sem).wait()
    out = x_ref[...]
    for k in range(TOP_K):
        out = out + route_ref[:, k:k + 1] * buf[k]
    o_ref[...] = _rms(out, g_ref[...])


def _combine(x, y, pos, route, gain, rows=COMBINE_ROWS):
    t, d = x.shape
    return pl.pallas_call(
        functools.partial(_combine_kernel, rows), grid=(t // rows,),
        in_specs=[pl.BlockSpec(memory_space=pl.ANY), pl.BlockSpec((rows, d), lambda i: (i, 0)),
                  pl.BlockSpec((rows, LANES), lambda i: (i, 0)), pl.BlockSpec((1, d), lambda i: (0, 0)),
                  pl.BlockSpec(memory_space=pl.ANY)],
        out_specs=pl.BlockSpec((rows, d), lambda i: (i, 0)),
        out_shape=jax.ShapeDtypeStruct((t, d), F32),
        scratch_shapes=[pltpu.SMEM((TOP_K * rows,), jnp.int32), pltpu.VMEM((TOP_K, rows, d), F32),
                        pltpu.SemaphoreType.DMA(()), pltpu.SemaphoreType.DMA(())],
        compiler_params=_params("arbitrary"), name="moe_combine",
    )(_slot_indices(pos, rows), x, route, gain.reshape(1, d), y)


def _swiglu_kernel(x_ref, g_ref, wg_ref, wu_ref, wd_ref, o_ref, h_ref, acc_ref):
    f = pl.program_id(1)

    @pl.when(f == 0)
    def _():
        h_ref[...] = _rms(x_ref[...], g_ref[...]).astype(BF16)
        acc_ref[...] = jnp.zeros_like(acc_ref)

    h = h_ref[...]
    gate = _dot(h, wg_ref[...])
    up = _dot(h, wu_ref[...])
    act = gate * jax.nn.sigmoid(gate) * up
    acc_ref[...] += _dot(act.astype(BF16), wd_ref[...])

    @pl.when(f == pl.num_programs(1) - 1)
    def _():
        o_ref[...] = x_ref[...] + acc_ref[...]


def _swiglu(x, gain, w_gate, w_up, w_down, tm=1024, tf=256):
    t, d = x.shape
    ff = w_gate.shape[1]
    row = pl.BlockSpec((tm, d), lambda i, f: (i, 0))
    return pl.pallas_call(
        _swiglu_kernel, grid=(t // tm, ff // tf),
        in_specs=[row, pl.BlockSpec((1, d), lambda i, f: (0, 0)),
                  pl.BlockSpec((d, tf), lambda i, f: (0, f)), pl.BlockSpec((d, tf), lambda i, f: (0, f)),
                  pl.BlockSpec((tf, d), lambda i, f: (f, 0))],
        out_specs=row, out_shape=jax.ShapeDtypeStruct((t, d), F32),
        scratch_shapes=[pltpu.VMEM((tm, d), BF16), pltpu.VMEM((tm, d), F32)],
        compiler_params=_params("parallel", "arbitrary"), name="swiglu",
    )(x, gain.reshape(1, d), w_gate, w_up, w_down)


def _even_layer(x, batch, seqlen, norm_mix, w_in, s5, w_glu, b_glu, d_skip, w_out, norm_ffn, w_gate, w_up, w_down):
    s5_width = d_skip.size
    sb_width = (w_in.shape[1] - s5_width) // 3
    scale = LOG2_E / math.sqrt(SB_HEAD_DIM)
    w_u = w_in[:, :s5_width].astype(BF16)
    w_q = (w_in[:, s5_width:s5_width + sb_width] * scale).astype(BF16)
    w_kv = w_in[:, s5_width + sb_width:].astype(BF16)
    u, qt, kv = _norm_proj(x, norm_mix, [w_u, w_q.T, w_kv], [F32, BF16, BF16], [False, True, False], batch)

    n_scan = int(math.log2(seqlen // S5_CHUNK))
    tables = _s5_tables(*s5, n_scan)
    y = _s5_conv(u, tables, batch, seqlen)
    a_out = _s5_glu(y, u, d_skip, w_glu, b_glu)

    b_out_t = _sb_attention(qt, kv.reshape(batch, seqlen, 2 * sb_width))

    w_out = w_out.astype(BF16)
    x = _proj_residual(x, [a_out, b_out_t], [w_out[:s5_width], w_out[s5_width:]], [False, True], batch)
    return _swiglu(x, norm_ffn, w_gate.astype(BF16), w_up.astype(BF16), w_down.astype(BF16))


def _odd_layer(x, batch, seqlen, norm_mix, w_in, b_f, w_out, norm_ffn, w_router, w_gate, w_up, w_down, final_gain):
    heads = b_f.size
    width = heads * FOX_HEAD_DIM
    scale = LOG2_E / math.sqrt(FOX_HEAD_DIM)
    d = x.shape[1]
    w_q = (w_in[:, :width] * scale).astype(BF16)
    w_kv = w_in[:, width:3 * width].astype(BF16)
    w_f = jnp.zeros((d, LANES), F32).at[:, :heads].set(w_in[:, 3 * width:]).astype(BF16)
    qt, kv, f_logit = _norm_proj(x, norm_mix, [w_q.T, w_kv, w_f], [BF16, BF16, F32], [True, False, False], batch)
    bias = jnp.zeros((1, LANES), F32).at[0, :heads].set(b_f)
    cum = _fox_gate_cumsum(f_logit, bias, batch, seqlen)

    c_out_t = _fox_attention(qt, kv.reshape(batch, seqlen, 2 * width), cum.reshape(batch, seqlen, LANES))
    x = _proj_residual(x, [c_out_t], [w_out.astype(BF16)], [True], batch)
    route = _router(x, norm_ffn, w_router)
    pos, visits = _route_plan(route, MOE_TILE)
    xs = _dispatch(x, pos)
    ys = _experts(xs, norm_ffn, visits, w_gate, w_up, w_down, MOE_TILE)
    return _combine(x, ys, pos, route, final_gain)


def kernel(x, ev_norm_mix, ev_w_in, s5_a_re, s5_a_im, s5_b_re, s5_b_im, s5_c_re, s5_c_im, s5_d, s5_log_step,
           s5_w_glu, s5_b_glu, ev_w_out, ev_norm_ffn, ffn_w_gate, ffn_w_up, ffn_w_down, od_norm_mix, od_w_in,
           fox_b_f, od_w_out, od_norm_ffn, moe_w_router, moe_w_gate, moe_w_up, moe_w_down, final_norm):
    batch, seqlen, d = x.shape
    assert ev_w_in.shape[0] == 1 and od_w_in.shape[0] == 1, "one even and one odd layer"
    h = x.reshape(batch * seqlen, d)
    s5 = (s5_a_re[0], s5_a_im[0], s5_b_re[0], s5_b_im[0], s5_c_re[0], s5_c_im[0], s5_log_step[0])
    h = _even_layer(h, batch, seqlen, ev_norm_mix[0], ev_w_in[0], s5, s5_w_glu[0], s5_b_glu[0], s5_d[0].reshape(-1),
                    ev_w_out[0], ev_norm_ffn[0], ffn_w_gate[0], ffn_w_up[0], ffn_w_down[0])
    h = _odd_layer(h, batch, seqlen, od_norm_mix[0], od_w_in[0], fox_b_f[0], od_w_out[0], od_norm_ffn[0],
                   moe_w_router[0], moe_w_gate[0], moe_w_up[0], moe_w_down[0], final_norm)
    return h.reshape(batch, seqlen, d)
```
